```python
import math, functools
import jax, jax.numpy as jnp
from jax import lax
import numpy as np

D_MODEL = 1024
BATCH = 4
SEQ = 4096
DEPTH = 1
DEC_BATCH = 128
DEC_SEQ = 1
PAST_LEN = 16384
PAGE_SIZE = 128

MLA_HEADS = 8
QK_NOPE = 64
QK_ROPE = 32
QK_DIM = QK_NOPE + QK_ROPE
V_DIM = 64
Q_LORA = 384
KV_LORA = 256
ROPE_THETA = 10000.0
Q_BLOCK = 128
ATTN_SCALE = 1.0 / math.sqrt(QK_DIM)
SSM_HEADS = 8
SSM_HEADDIM = 64
SSM_INNER = SSM_HEADS * SSM_HEADDIM
SSM_GROUPS = 2
D_STATE = 128
CONV_K = 4
CONV_CH = SSM_INNER + 2 * SSM_GROUPS * D_STATE
SSD_CHUNK = 128
MIX_WIDTH = MLA_HEADS * V_DIM + SSM_INNER
IN_SPLITS = [Q_LORA, KV_LORA, QK_ROPE, SSM_INNER, CONV_CH, SSM_HEADS]
IN_COLS = sum(IN_SPLITS)
N_GROUPS = 4
EXPERTS_PER_GROUP = 4
N_EXPERTS = N_GROUPS * EXPERTS_PER_GROUP
EXPERT_FF = 512
TOP_K = 2
PLE_DIM = 256
EPS = 1e-6

kernel_name = "hymba_mla_ssd_hmoe_step"


def rmsnorm(x, g):
    xf = x.astype(jnp.float32)
    xf = xf * lax.rsqrt(jnp.mean(xf * xf, axis=-1, keepdims=True) + EPS)
    return xf.astype(x.dtype) * g


def rope(x, pos):
    half = x.shape[-1] // 2
    inv = ROPE_THETA ** (-jnp.arange(half, dtype=jnp.float32) / half)
    ang = pos.astype(jnp.float32)[:, None] * inv[None, :]
    shape = (1, pos.shape[0]) + (1,) * (x.ndim - 3) + (half,)
    cos = jnp.cos(ang).reshape(shape).astype(x.dtype)
    sin = jnp.sin(ang).reshape(shape).astype(x.dtype)
    x1, x2 = x[..., :half], x[..., half:]
    return jnp.concatenate([x1 * cos - x2 * sin, x1 * sin + x2 * cos], axis=-1)


def mla_prompt_attn(q_nope, q_rope, ckv, krope, w_uk, w_uv):
    B, S, H, _ = q_nope.shape
    k_nope = jnp.einsum('bsc,chd->bshd', ckv, w_uk)
    v = jnp.einsum('bsc,chd->bshd', ckv, w_uv)
    q = jnp.concatenate([q_nope, q_rope], axis=-1)
    k = jnp.concatenate([k_nope, jnp.broadcast_to(krope[:, :, None, :], (B, S, H, QK_ROPE))], axis=-1)
    nb = S // Q_BLOCK
    qb = q.reshape(B, nb, Q_BLOCK, H, QK_DIM).transpose(1, 0, 2, 3, 4)
    kpos = jnp.arange(S)

    def block(args):
        qi, i = args
        s = jnp.einsum('bqhd,bkhd->bhqk', qi, k).astype(jnp.float32) * ATTN_SCALE
        qpos = i * Q_BLOCK + jnp.arange(Q_BLOCK)
        s = jnp.where(kpos[None, :] <= qpos[:, None], s, -jnp.inf)
        p = jax.nn.softmax(s, axis=-1).astype(v.dtype)
        return jnp.einsum('bhqk,bkhd->bqhd', p, v)

    o = lax.map(block, (qb, jnp.arange(nb)))
    return o.transpose(1, 0, 2, 3, 4).reshape(B, S, H, V_DIM)


def mla_paged_attn(q_nope, q_rope, ckv, krope, w_uk, w_uv, cache_ckv, cache_krope, page_table):
    B, L, H, _ = q_nope.shape
    past_ckv = cache_ckv[page_table].reshape(B, -1, KV_LORA)
    past_kr = cache_krope[page_table].reshape(B, -1, QK_ROPE)
    T = past_ckv.shape[1]
    q_lat = jnp.einsum('blhd,chd->blhc', q_nope, w_uk)
    s_past = (jnp.einsum('blhc,btc->bhlt', q_lat, past_ckv)
              + jnp.einsum('blhr,btr->bhlt', q_rope, past_kr)).astype(jnp.float32) * ATTN_SCALE
    s_new = (jnp.einsum('blhc,bsc->bhls', q_lat, ckv)
             + jnp.einsum('blhr,bsr->bhls', q_rope, krope)).astype(jnp.float32) * ATTN_SCALE
    s_new = jnp.where(jnp.tril(jnp.ones((L, L), bool)), s_new, -jnp.inf)
    p = jax.nn.softmax(jnp.concatenate([s_past, s_new], axis=-1), axis=-1).astype(ckv.dtype)
    o_lat = (jnp.einsum('bhlt,btc->blhc', p[..., :T], past_ckv)
             + jnp.einsum('bhls,bsc->blhc', p[..., T:], ckv))
    return jnp.einsum('blhc,chd->blhd', o_lat, w_uv)


def causal_conv(xbc, prev, w, b):
    L = xbc.shape[1]
    xp = jnp.concatenate([prev.astype(xbc.dtype), xbc], axis=1)
    y = sum(xp[:, k:k + L] * w[k] for k in range(CONV_K)) + b
    return y, xp[:, xp.shape[1] - (CONV_K - 1):]


def segsum(a):
    T = a.shape[-1]
    a_rep = jnp.broadcast_to(a[..., None], a.shape + (T,))
    a_rep = jnp.where(jnp.tril(jnp.ones((T, T), bool), -1), a_rep, 0.0)
    s = jnp.cumsum(a_rep, axis=-2)
    return jnp.where(jnp.tril(jnp.ones((T, T), bool)), s, -jnp.inf)


def ssd(x, dt, a, bm, cm, h0, chunk):
    Bsz, L, H, P = x.shape
    N = bm.shape[-1]
    nc = L // chunk
    x = x.astype(jnp.float32).reshape(Bsz, nc, chunk, H, P)
    dt = dt.astype(jnp.float32).reshape(Bsz, nc, chunk, H)
    bm = bm.astype(jnp.float32).reshape(Bsz, nc, chunk, H, N)
    cm = cm.astype(jnp.float32).reshape(Bsz, nc, chunk, H, N)
    dt_h = jnp.moveaxis(dt, -1, 1)
    da = dt_h * a.astype(jnp.float32)[None, :, None, None]
    da_cs = jnp.cumsum(da, axis=-1)
    m = jnp.einsum('bclhn,bcshn->bhcls', cm, bm) * jnp.exp(segsum(da)) * dt_h[:, :, :, None, :]
    y_diag = jnp.einsum('bhcls,bcshp->bclhp', m, x)
    w_end = jnp.exp(da_cs[..., -1:] - da_cs) * dt_h
    states = jnp.einsum('bclhn,bhcl,bclhp->bchpn', bm, w_end, x)
    states = jnp.concatenate([h0.astype(jnp.float32)[:, None], states], axis=1)
    chunk_decay = jnp.exp(segsum(jnp.pad(da_cs[..., -1], ((0, 0), (0, 0), (1, 0)))))
    new_states = jnp.einsum('bhzc,bchpn->bzhpn', chunk_decay, states)
    y_off = jnp.einsum('bclhn,bchpn,bhcl->bclhp', cm, new_states[:, :-1], jnp.exp(da_cs))
    return (y_diag + y_off).reshape(Bsz, L, H, P), new_states[:, -1]


def mixer_block(h, pos, attend, conv_prev, ssm_h0, lw):
    Bsz, L, _ = h.shape
    u = h @ lw['w_in']
    c_q, c_kv, k_r, z, xbc, dt = jnp.split(u, list(np.cumsum(IN_SPLITS[:-1])), axis=-1)
    q = jnp.einsum('blc,chd->blhd', rmsnorm(c_q, lw['norm_q']), lw['w_uq'])
    q_nope, q_rope = q[..., :QK_NOPE], rope(q[..., QK_NOPE:], pos)
    c_kv = rmsnorm(c_kv, lw['norm_kv'])
    k_r = rope(k_r, pos)
    w_uk, w_uv = lw['w_ukv'][..., :QK_NOPE], lw['w_ukv'][..., QK_NOPE:]
    o_attn = attend(q_nope, q_rope, c_kv, k_r, w_uk, w_uv).reshape(Bsz, L, MLA_HEADS * V_DIM)
    o_attn = rmsnorm(o_attn, lw['norm_attn_out'])
    xbc, conv_new = causal_conv(xbc, conv_prev, lw['conv_w'], lw['conv_b'])
    xbc = jax.nn.silu(xbc)
    xs, bm, cm = jnp.split(xbc, [SSM_INNER, SSM_INNER + SSM_GROUPS * D_STATE], axis=-1)
    xs = xs.reshape(Bsz, L, SSM_HEADS, SSM_HEADDIM)
    rep = SSM_HEADS // SSM_GROUPS
    bm = jnp.repeat(bm.reshape(Bsz, L, SSM_GROUPS, D_STATE), rep, axis=2)
    cm = jnp.repeat(cm.reshape(Bsz, L, SSM_GROUPS, D_STATE), rep, axis=2)
    dt = jax.nn.softplus((dt + lw['dt_bias']).astype(jnp.float32))
    a = -jnp.exp(lw['a_log'].astype(jnp.float32))
    chunk = SSD_CHUNK if L % SSD_CHUNK == 0 else L
    y, ssm_new = ssd(xs, dt, a, bm, cm, ssm_h0, chunk)
    y = y.astype(h.dtype) + lw['d_skip'][:, None] * xs
    y = rmsnorm(y.reshape(Bsz, L, SSM_INNER) * jax.nn.silu(z), lw['norm_ssm_out'])
    mix = jnp.concatenate([o_attn, y], axis=-1) @ lw['w_out']
    return mix, (c_kv, k_r, conv_new, ssm_new.astype(h.dtype))


def hier_moe(h, lw):
    T = h.shape[0]
    g_prob = jax.nn.softmax((h @ lw['w_group_router'] + lw['b_group_router']).astype(jnp.float32), axis=-1)
    g_w, g_idx = lax.top_k(g_prob, 1)
    e_logits = (h @ lw['w_expert_router'] + lw['b_expert_router']).astype(jnp.float32)
    e_logits = jnp.take_along_axis(e_logits.reshape(T, N_GROUPS, EXPERTS_PER_GROUP), g_idx[:, :, None], axis=1)[:, 0]
    e_w, e_idx = lax.top_k(jax.nn.softmax(e_logits, axis=-1), TOP_K)
    comb = g_w * (e_w / jnp.sum(e_w, axis=-1, keepdims=True))
    gidx = g_idx * EXPERTS_PER_GROUP + e_idx
    dense = jnp.einsum('tk,tke->te', comb, jax.nn.one_hot(gidx, N_EXPERTS, dtype=jnp.float32)).astype(h.dtype)
    out = jnp.zeros_like(h)
    for e in range(N_EXPERTS):
        act = jax.nn.silu(h @ lw['w_exp_gate'][e]) * (h @ lw['w_exp_up'][e])
        out = out + dense[:, e:e + 1] * (act @ lw['w_exp_down'][e])
    return out


def hybrid_layer(x, p, pos, attend, conv_prev, ssm_h0, lw):
    mix, st = mixer_block(rmsnorm(x, lw['norm_mix']), pos, attend, conv_prev, ssm_h0, lw)
    x = x + mix
    h = rmsnorm(x, lw['norm_ffn'])
    x = x + hier_moe(h.reshape(-1, D_MODEL), lw).reshape(x.shape)
    gate = jax.nn.sigmoid(rmsnorm(x, lw['norm_ple']) @ lw['w_ple_gate'])
    x = x + gate * (p @ lw['w_ple_proj'])
    return x, st


def setup_inputs(seed: int = 0) -> dict:
    key = jax.random.key(seed)
    ks = iter(jax.random.split(key, 48))

    def nrm(shape, scale=1.0):
        return jax.random.normal(next(ks), shape, jnp.float32) * scale

    def gain(shape):
        return 1.0 + nrm(shape, 0.01)

    n_pages = PAST_LEN // PAGE_SIZE
    n_pool = (DEC_BATCH * n_pages * 5) // 4
    page_table = jax.random.permutation(next(ks), n_pool)[:DEC_BATCH * n_pages].reshape(DEC_BATCH, n_pages).astype(jnp.int32)
    dt0 = jnp.exp(jax.random.uniform(next(ks), (DEPTH, SSM_HEADS), jnp.float32, math.log(1e-3), math.log(1e-1)))
    dt_bias = dt0 + jnp.log(-jnp.expm1(-dt0))
    a_log = jnp.log(jax.random.uniform(next(ks), (DEPTH, SSM_HEADS), jnp.float32, 1.0, 16.0))
    return {
        'x_prompt': nrm((BATCH, SEQ, D_MODEL)),
        'x_sample': nrm((DEC_BATCH, DEC_SEQ, D_MODEL)),
        'p_prompt': nrm((DEPTH, BATCH, SEQ, PLE_DIM)),
        'p_sample': nrm((DEPTH, DEC_BATCH, DEC_SEQ, PLE_DIM)),
        'cache_ckv': nrm((DEPTH, n_pool, PAGE_SIZE, KV_LORA)),
        'cache_krope': nrm((DEPTH, n_pool, PAGE_SIZE, QK_ROPE)),
        'state_conv': nrm((DEPTH, DEC_BATCH, CONV_K - 1, CONV_CH)),
        'state_ssm': nrm((DEPTH, DEC_BATCH, SSM_HEADS, SSM_HEADDIM, D_STATE), 0.5),
        'page_table': page_table,
        'norm_mix': gain((DEPTH, D_MODEL)),
        'w_in': nrm((DEPTH, D_MODEL, IN_COLS), D_MODEL ** -0.5),
        'norm_q': gain((DEPTH, Q_LORA)),
        'w_uq': nrm((DEPTH, Q_LORA, MLA_HEADS, QK_DIM), Q_LORA ** -0.5),
        'norm_kv': gain((DEPTH, KV_LORA)),
        'w_ukv': nrm((DEPTH, KV_LORA, MLA_HEADS, QK_NOPE + V_DIM), KV_LORA ** -0.5),
        'norm_attn_out': gain((DEPTH, MLA_HEADS * V_DIM)),
        'conv_w': nrm((DEPTH, CONV_K, CONV_CH), CONV_K ** -0.5),
        'conv_b': nrm((DEPTH, CONV_CH), 0.01),
        'dt_bias': dt_bias,
        'a_log': a_log,
        'd_skip': gain((DEPTH, SSM_HEADS)),
        'norm_ssm_out': gain((DEPTH, SSM_INNER)),
        'w_out': nrm((DEPTH, MIX_WIDTH, D_MODEL), MIX_WIDTH ** -0.5),
        'norm_ffn': gain((DEPTH, D_MODEL)),
        'w_group_router': nrm((DEPTH, D_MODEL, N_GROUPS), D_MODEL ** -0.5),
        'b_group_router': nrm((DEPTH, N_GROUPS), 0.01),
        'w_expert_router': nrm((DEPTH, D_MODEL, N_EXPERTS), D_MODEL ** -0.5),
        'b_expert_router': nrm((DEPTH, N_EXPERTS), 0.01),
        'w_exp_gate': nrm((DEPTH, N_EXPERTS, D_MODEL, EXPERT_FF), D_MODEL ** -0.5),
        'w_exp_up': nrm((DEPTH, N_EXPERTS, D_MODEL, EXPERT_FF), D_MODEL ** -0.5),
        'w_exp_down': nrm((DEPTH, N_EXPERTS, EXPERT_FF, D_MODEL), EXPERT_FF ** -0.5),
        'norm_ple': gain((DEPTH, D_MODEL)),
        'w_ple_gate': nrm((DEPTH, D_MODEL, D_MODEL), D_MODEL ** -0.5),
        'w_ple_proj': nrm((DEPTH, PLE_DIM, D_MODEL), PLE_DIM ** -0.5),
        'norm_final': gain((D_MODEL,)),
    }


def reference(x_prompt, x_sample, p_prompt, p_sample, cache_ckv, cache_krope, state_conv, state_ssm,
              page_table, norm_mix, w_in, norm_q, w_uq, norm_kv, w_ukv, norm_attn_out, conv_w, conv_b,
              dt_bias, a_log, d_skip, norm_ssm_out, w_out, norm_ffn, w_group_router, b_group_router,
              w_expert_router, b_expert_router, w_exp_gate, w_exp_up, w_exp_down, norm_ple, w_ple_gate,
              w_ple_proj, norm_final):
    bp, sp = x_prompt.shape[0], x_prompt.shape[1]
    bs, ls = x_sample.shape[0], x_sample.shape[1]
    pos_p = jnp.arange(sp)
    pos_s = PAST_LEN + jnp.arange(ls)
    conv0 = jnp.zeros((bp, CONV_K - 1, CONV_CH), x_prompt.dtype)
    ssm0 = jnp.zeros((bp, SSM_HEADS, SSM_HEADDIM, D_STATE), x_prompt.dtype)
    xp, xs = x_prompt, x_sample
    st_p_all, st_s_all = [], []
    for i in range(DEPTH):
        lw = dict(w_in=w_in[i], norm_mix=norm_mix[i], norm_q=norm_q[i], w_uq=w_uq[i], norm_kv=norm_kv[i],
                  w_ukv=w_ukv[i], norm_attn_out=norm_attn_out[i], conv_w=conv_w[i], conv_b=conv_b[i],
                  dt_bias=dt_bias[i], a_log=a_log[i], d_skip=d_skip[i], norm_ssm_out=norm_ssm_out[i],
                  w_out=w_out[i], norm_ffn=norm_ffn[i], w_group_router=w_group_router[i],
                  b_group_router=b_group_router[i], w_expert_router=w_expert_router[i],
                  b_expert_router=b_expert_router[i], w_exp_gate=w_exp_gate[i], w_exp_up=w_exp_up[i],
                  w_exp_down=w_exp_down[i], norm_ple=norm_ple[i], w_ple_gate=w_ple_gate[i],
                  w_ple_proj=w_ple_proj[i])
        xp, st_p = hybrid_layer(xp, p_prompt[i], pos_p, mla_prompt_attn, conv0, ssm0, lw)
        attend_s = functools.partial(mla_paged_attn, cache_ckv=cache_ckv[i], cache_krope=cache_krope[i],
                                     page_table=page_table)
        xs, st_s = hybrid_layer(xs, p_sample[i], pos_s, attend_s, state_conv[i], state_ssm[i], lw)
        st_p_all.append(st_p)
        st_s_all.append(st_s)
    ckv_p, kr_p, conv_p, ssm_p = [jnp.stack([s[j] for s in st_p_all]) for j in range(4)]
    ckv_s, kr_s, conv_s, ssm_s = [jnp.stack([s[j] for s in st_s_all]) for j in range(4)]
    y_prompt = rmsnorm(xp, norm_final)
    y_sample = rmsnorm(xs, norm_final)
    return (y_prompt, y_sample, ckv_p, kr_p, conv_p, ssm_p, ckv_s, kr_s, conv_s, ssm_s)
```

```python
import functools
import math

import jax
import jax.numpy as jnp
from jax import lax
from jax.experimental import pallas as pl
from jax.experimental.pallas import tpu as pltpu

F32 = jnp.float32
BF16 = jnp.bfloat16

MLA_HEADS = 8
QK_NOPE = 64
QK_ROPE = 32
V_DIM = 64
Q_LORA = 384
KV_LORA = 256
ROPE_THETA = 10000.0
ATTN_SCALE = 1.0 / math.sqrt(QK_NOPE + QK_ROPE)
SSM_HEADS = 8
SSM_HEADDIM = 64
SSM_INNER = SSM_HEADS * SSM_HEADDIM
SSM_GROUPS = 2
D_STATE = 128
CONV_K = 4
CONV_CH = SSM_INNER + 2 * SSM_GROUPS * D_STATE
SSD_CHUNK = 128
N_GROUPS = 4
EXPERTS_PER_GROUP = 4
N_EXPERTS = N_GROUPS * EXPERTS_PER_GROUP
EXPERT_FF = 512
EPS = 1e-6

LANE = 128
HEAD_PAD = 128
ROPE_LANE0 = QK_NOPE
Q_ABS = KV_LORA + LANE

C_CQ = 0
C_CKV = C_CQ + Q_LORA
C_KRA = C_CKV + KV_LORA
C_KRB = C_KRA + LANE
C_Z = C_KRB + LANE
C_XBC = C_Z + SSM_INNER
C_DT = C_XBC + CONV_CH
IN_COLS_P = C_DT + LANE

VMEM_LIMIT = 56 * 1024 * 1024


def _cparams(sem, vmem=VMEM_LIMIT):
    return pltpu.CompilerParams(dimension_semantics=sem, vmem_limit_bytes=vmem)


def _rms(x, g):
    return x * lax.rsqrt(jnp.mean(x * x, axis=-1, keepdims=True) + EPS) * g


def _dot(a, b):
    return jnp.dot(a, b, preferred_element_type=F32)


def _dot_nt(a, b):
    return lax.dot_general(a, b, (((1,), (1,)), ((), ())), preferred_element_type=F32)


def _dot_tn(a, b):
    return lax.dot_general(a, b, (((0,), (0,)), ((), ())), preferred_element_type=F32)


def _silu(x):
    return x * (1.0 / (1.0 + jnp.exp(-x)))


def _softplus(x):
    return jnp.maximum(x, 0.0) + jnp.log(1.0 + jnp.exp(-jnp.abs(x)))


def _full(shape):
    nd = len(shape)
    return pl.BlockSpec(shape, lambda *a: (0,) * nd)


def _in_proj_kernel(x_ref, gmix_ref, win_ref, gq_ref, wqa_ref, wqb_ref, gkv_ref, wkv_ref, cos_ref, sin_ref,
                    *refs, absorb):
    if absorb:
        wabs_ref, q_ref, ckv_ref, krp_ref, z_ref, xbc_ref, dt_ref = refs
    else:
        q_ref, k_ref, v_ref, ckv_ref, krp_ref, z_ref, xbc_ref, dt_ref = refs
    h = _rms(x_ref[...], gmix_ref[...]).astype(BF16)
    u = _dot(h, win_ref[...])
    cq = _rms(u[:, C_CQ:C_CQ + Q_LORA], gq_ref[...]).astype(BF16)
    qa = _dot(cq, wqa_ref[...])
    qb = _dot(cq, wqb_ref[...])
    cos = cos_ref[...]
    sin = sin_ref[...]
    lane = lax.broadcasted_iota(jnp.int32, (1, LANE), 1)
    q_cos = ATTN_SCALE * (jnp.where(lane < QK_NOPE, 1.0, 0.0) + cos)
    q_sin = ATTN_SCALE * sin
    ckv = _rms(u[:, C_CKV:C_CKV + KV_LORA], gkv_ref[...])
    ckv_ref[...] = ckv
    krp = u[:, C_KRA:C_KRA + LANE] * cos + u[:, C_KRB:C_KRB + LANE] * sin
    krp_ref[...] = krp
    z_ref[...] = u[:, C_Z:C_Z + SSM_INNER]
    xbc_ref[...] = u[:, C_XBC:C_XBC + CONV_CH]
    dt_ref[...] = u[:, C_DT:C_DT + LANE]
    for hd in range(MLA_HEADS):
        sl = slice(hd * HEAD_PAD, (hd + 1) * HEAD_PAD)
        qh = (qa[:, sl] * q_cos + qb[:, sl] * q_sin).astype(BF16)
        if absorb:
            q_ref[:, hd * Q_ABS:(hd + 1) * Q_ABS] = _dot(qh, wabs_ref[hd]).astype(BF16)
        else:
            q_ref[:, sl] = qh
    if not absorb:
        kv = _dot(ckv.astype(BF16), wkv_ref[...])
        for hd in range(MLA_HEADS):
            sl = slice(hd * HEAD_PAD, (hd + 1) * HEAD_PAD)
            k_ref[:, sl] = (kv[:, sl] + krp).astype(BF16)
        v_ref[...] = kv[:, MLA_HEADS * HEAD_PAD:].astype(BF16)


def _in_proj(x, cos, sin, w, *, absorb, tm, n_pos_blocks):
    t, d = x.shape
    nt = t // tm
    row = lambda i: (i, 0)
    pos = lambda i: (i % n_pos_blocks, 0)
    in_specs = [
        pl.BlockSpec((tm, d), row), _full((1, d)), _full((d, IN_COLS_P)), _full((1, Q_LORA)),
        _full((Q_LORA, MLA_HEADS * HEAD_PAD)), _full((Q_LORA, MLA_HEADS * HEAD_PAD)), _full((1, KV_LORA)),
        _full((KV_LORA, MLA_HEADS * (HEAD_PAD + V_DIM))), pl.BlockSpec((tm, LANE), pos), pl.BlockSpec((tm, LANE), pos),
    ]
    args = [x, w['norm_mix'], w['w_in_p'], w['norm_q'], w['wq_a'], w['wq_b'], w['norm_kv'], w['w_kv_p'], cos, sin]
    tail_shapes = [
        jax.ShapeDtypeStruct((t, KV_LORA), F32), jax.ShapeDtypeStruct((t, LANE), F32),
        jax.ShapeDtypeStruct((t, SSM_INNER), F32), jax.ShapeDtypeStruct((t, CONV_CH), F32),
        jax.ShapeDtypeStruct((t, LANE), F32),
    ]
    tail_specs = [pl.BlockSpec((tm, KV_LORA), row), pl.BlockSpec((tm, LANE), row), pl.BlockSpec((tm, SSM_INNER), row),
                  pl.BlockSpec((tm, CONV_CH), row), pl.BlockSpec((tm, LANE), row)]
    if absorb:
        in_specs.append(_full((MLA_HEADS, HEAD_PAD, Q_ABS)))
        args.append(w['w_abs'])
        out_shape = [jax.ShapeDtypeStruct((t, MLA_HEADS * Q_ABS), BF16)] + tail_shapes
        out_specs = [pl.BlockSpec((tm, MLA_HEADS * Q_ABS), row)] + tail_specs
    else:
        out_shape = [jax.ShapeDtypeStruct((t, MLA_HEADS * HEAD_PAD), BF16),
                     jax.ShapeDtypeStruct((t, MLA_HEADS * HEAD_PAD), BF16),
                     jax.ShapeDtypeStruct((t, MLA_HEADS * V_DIM), BF16)] + tail_shapes
        out_specs = [pl.BlockSpec((tm, MLA_HEADS * HEAD_PAD), row), pl.BlockSpec((tm, MLA_HEADS * HEAD_PAD), row),
                     pl.BlockSpec((tm, MLA_HEADS * V_DIM), row)] + tail_specs
    return pl.pallas_call(
        functools.partial(_in_proj_kernel, absorb=absorb),
        grid=(nt,), in_specs=in_specs, out_specs=out_specs, out_shape=out_shape,
        compiler_params=_cparams(("parallel",)), name="in_proj_abs" if absorb else "in_proj",
    )(*args)


def _flash_kernel(q_ref, k_ref, v_ref, o_ref, *, tq, tk):
    qi = pl.program_id(2)
    lane = lax.broadcasted_iota(jnp.int32, (1, LANE), 1)
    outs = []
    for hh in range(2):
        hsl = slice(hh * HEAD_PAD, (hh + 1) * HEAD_PAD)
        q = q_ref[:, hsl]

        def step(ki, carry, masked):
            m, l, acc = carry
            ks = pl.multiple_of(ki * tk, tk)
            s = _dot_nt(q, k_ref[pl.ds(ks, tk), hsl])
            if masked:
                row = qi * tq + lax.broadcasted_iota(jnp.int32, (tq, tk), 0)
                col = ks + lax.broadcasted_iota(jnp.int32, (tq, tk), 1)
                s = jnp.where(col <= row, s, -jnp.inf)
            m_new = jnp.maximum(m, jnp.max(s, axis=-1, keepdims=True))
            alpha = jnp.exp(m - m_new)
            p = jnp.exp(s - m_new)
            l = alpha * l + jnp.sum(p, axis=-1, keepdims=True)
            acc = alpha * acc + _dot(p.astype(BF16), v_ref[pl.ds(ks, tk), :])
            return m_new, l, acc

        init = (jnp.full((tq, 1), -jnp.inf, F32), jnp.zeros((tq, 1), F32), jnp.zeros((tq, 2 * V_DIM), F32))
        n_full = qi * (tq // tk)
        carry = lax.fori_loop(0, n_full, functools.partial(step, masked=False), init)
        for d in range(tq // tk):
            carry = step(n_full + d, carry, True)
        _, l, acc = carry
        outs.append(acc / l)
    o_ref[...] = jnp.where(lane < V_DIM, outs[0], outs[1])


def _flash(q, k, v, *, batch, seq, tq, tk):
    t = batch * seq
    nq = seq // tq
    npair = MLA_HEADS // 2
    return pl.pallas_call(
        functools.partial(_flash_kernel, tq=tq, tk=tk),
        grid=(batch, npair, nq),
        in_specs=[pl.BlockSpec((tq, 2 * HEAD_PAD), lambda b, hp, i: (b * nq + i, hp)),
                  pl.BlockSpec((seq, 2 * HEAD_PAD), lambda b, hp, i: (b, hp)),
                  pl.BlockSpec((seq, 2 * V_DIM), lambda b, hp, i: (b, hp))],
        out_specs=pl.BlockSpec((tq, 2 * V_DIM), lambda b, hp, i: (b * nq + i, hp)),
        out_shape=jax.ShapeDtypeStruct((t, MLA_HEADS * V_DIM), F32),
        compiler_params=_cparams(("parallel", "parallel", "arbitrary")), name="flash",
    )(q, k, v)


def _paged_kernel(pt_ref, q_ref, ckvn_ref, krn_ref, cc_hbm, ck_hbm, o_ref, cbuf, kbuf, sem, m_sc, l_sc, acc_sc,
                  *, n_chunks, ch, page, nslot, n_batch):
    b = pl.program_id(0)
    total = n_batch * n_chunks

    def copies(g, slot):
        gb = g // n_chunks
        gc = g % n_chunks
        out = []
        for j in range(ch):
            pg = pt_ref[gb, gc * ch + j]
            out.append(pltpu.make_async_copy(cc_hbm.at[pg], cbuf.at[slot, pl.ds(j * page, page), :], sem.at[0, slot]))
            out.append(pltpu.make_async_copy(ck_hbm.at[pg], kbuf.at[slot, pl.ds(j * page, page), :], sem.at[1, slot]))
        return out

    def start(g):
        @pl.when(g < total)
        def _():
            for c in copies(g, g % nslot):
                c.start()

    @pl.when(b == 0)
    def _():
        for g0 in range(nslot - 1):
            start(jnp.int32(g0))

    q = q_ref[0]
    q_lat = q[:, :KV_LORA]
    q_rope = q[:, KV_LORA:KV_LORA + QK_ROPE]
    m_sc[...] = jnp.full(m_sc.shape, -jnp.inf, F32)
    l_sc[...] = jnp.zeros(l_sc.shape, F32)
    acc_sc[...] = jnp.zeros(acc_sc.shape, F32)

    def chunk(c, carry):
        g = b * n_chunks + c
        start(g + nslot - 1)
        slot = g % nslot
        for cp in copies(g, slot):
            cp.wait()
        kc = cbuf[slot].astype(BF16)
        kr = kbuf[slot].astype(BF16)
        s = _dot_nt(q_lat, kc) + _dot_nt(q_rope, kr)
        m = m_sc[...]
        m_new = jnp.maximum(m, jnp.max(s, axis=-1, keepdims=True))
        alpha = jnp.exp(m - m_new)
        p = jnp.exp(s - m_new)
        l_sc[...] = alpha * l_sc[...] + jnp.sum(p, axis=-1, keepdims=True)
        acc_sc[...] = alpha * acc_sc[...] + _dot(p.astype(BF16), kc)
        m_sc[...] = m_new
        return carry

    lax.fori_loop(0, n_chunks, chunk, 0)

    ckvn = ckvn_ref[0]
    krn = krn_ref[0][:, ROPE_LANE0:ROPE_LANE0 + QK_ROPE]
    s_new = (jnp.sum(q_lat.astype(F32) * ckvn, axis=-1, keepdims=True)
             + jnp.sum(q_rope.astype(F32) * krn, axis=-1, keepdims=True))
    m = m_sc[...]
    m_new = jnp.maximum(m, s_new)
    alpha = jnp.exp(m - m_new)
    p_new = jnp.exp(s_new - m_new)
    l = alpha * l_sc[...] + p_new
    o_ref[0] = (alpha * acc_sc[...] + p_new * ckvn) / l


def _paged(q_abs, ckv_new, krp_new, cache_ckv, cache_kr, page_table, *, ch, nslot):
    nb, n_pages = page_table.shape
    page = cache_ckv.shape[1]
    n_chunks = n_pages // ch
    grid_spec = pltpu.PrefetchScalarGridSpec(
        num_scalar_prefetch=1, grid=(nb,),
        in_specs=[pl.BlockSpec((1, MLA_HEADS, Q_ABS), lambda b, pt: (b, 0, 0)),
                  pl.BlockSpec((1, 1, KV_LORA), lambda b, pt: (b, 0, 0)),
                  pl.BlockSpec((1, 1, LANE), lambda b, pt: (b, 0, 0)),
                  pl.BlockSpec(memory_space=pl.ANY), pl.BlockSpec(memory_space=pl.ANY)],
        out_specs=pl.BlockSpec((1, MLA_HEADS, KV_LORA), lambda b, pt: (b, 0, 0)),
        scratch_shapes=[pltpu.VMEM((nslot, ch * page, KV_LORA), F32), pltpu.VMEM((nslot, ch * page, QK_ROPE), F32),
                        pltpu.SemaphoreType.DMA((2, nslot)), pltpu.VMEM((MLA_HEADS, 1), F32),
                        pltpu.VMEM((MLA_HEADS, 1), F32), pltpu.VMEM((MLA_HEADS, KV_LORA), F32)])
    return pl.pallas_call(
        functools.partial(_paged_kernel, n_chunks=n_chunks, ch=ch, page=page, nslot=nslot, n_batch=nb),
        grid_spec=grid_spec, out_shape=jax.ShapeDtypeStruct((nb, MLA_HEADS, KV_LORA), F32),
        compiler_params=_cparams(("arbitrary",)), name="paged_attn",
    )(page_table, q_abs, ckv_new, krp_new, cache_ckv, cache_kr)


def _gated_norm(y, z, g):
    return _rms(y * _silu(z), g)


def _ssd_chunk_kernel(xbc_ref, z_ref, dt_ref, cw_ref, cb_ref, dtb_ref, alog_ref, dskip_ref, gn_ref,
                      y_ref, st_ref, xp_sc, *, chunk):
    c = pl.program_id(1)
    halo = 8

    @pl.when(c == 0)
    def _():
        xp_sc[0:halo, :] = jnp.zeros((halo, CONV_CH), F32)
        st_ref[...] = jnp.zeros(st_ref.shape, F32)

    xp_sc[halo:halo + chunk, :] = xbc_ref[...]
    conv = cb_ref[...]
    for kk in range(CONV_K):
        off = halo - (CONV_K - 1) + kk
        conv = conv + xp_sc[off:off + chunk, :] * cw_ref[kk:kk + 1, :]
    xp_sc[0:halo, :] = xbc_ref[chunk - halo:chunk, :]
    xbc = _silu(conv)
    xs = xbc[:, :SSM_INNER]
    dt = _softplus(dt_ref[...] + dtb_ref[...])
    da = dt * (-jnp.exp(alog_ref[...]))
    ri = lax.broadcasted_iota(jnp.int32, (chunk, chunk), 0)
    ci = lax.broadcasted_iota(jnp.int32, (chunk, chunk), 1)
    tril = ci <= ri
    cs = jnp.dot(jnp.where(tril, 1.0, 0.0), da, precision=lax.Precision.HIGHEST, preferred_element_type=F32)
    cs_t = cs.T
    dt_t = dt.T
    lane = lax.broadcasted_iota(jnp.int32, (1, LANE), 1)
    rowi = lax.broadcasted_iota(jnp.int32, (LANE, 1), 0)
    lo = lane < SSM_HEADDIM
    heads_per_group = SSM_HEADS // SSM_GROUPS
    ys = []
    for pr in range(SSM_HEADS // 2):
        g = (2 * pr) // heads_per_group
        bm = xbc[:, SSM_INNER + g * D_STATE:SSM_INNER + (g + 1) * D_STATE].astype(BF16)
        cm = xbc[:, SSM_INNER + (SSM_GROUPS + g) * D_STATE:SSM_INNER + (SSM_GROUPS + g + 1) * D_STATE].astype(BF16)
        cb = _dot_nt(cm, bm)
        xp = xs[:, pr * LANE:(pr + 1) * LANE]
        ms, xm, ecols, wcols, dlast, dsk = [], [], [], [], [], []
        for hh in range(2):
            hd = 2 * pr + hh
            col = cs[:, hd:hd + 1]
            rowv = cs_t[hd:hd + 1, :]
            seg = jnp.where(tril, col - rowv, -jnp.inf)
            ms.append((cb * jnp.exp(seg) * dt_t[hd:hd + 1, :]).astype(BF16))
            xm.append(jnp.where(lo if hh == 0 else ~lo, xp, 0.0).astype(BF16))
            last = cs[chunk - 1:chunk, hd:hd + 1]
            ecols.append(jnp.exp(col))
            wcols.append(jnp.exp(last - col) * dt[:, hd:hd + 1])
            dlast.append(jnp.exp(last))
            dsk.append(dskip_ref[:, hd:hd + 1])
        y_diag = _dot(jnp.concatenate(ms, axis=1), jnp.concatenate(xm, axis=0))
        st = st_ref[0, pr * LANE:(pr + 1) * LANE, :]
        y_off = _dot_nt(cm, st.astype(BF16)) * jnp.where(lo, ecols[0], ecols[1])
        xw = (xp * jnp.where(lo, wcols[0], wcols[1])).astype(BF16)
        st_new = _dot_tn(xw, bm)
        decay = jnp.where(rowi < SSM_HEADDIM, dlast[0], dlast[1])
        st_ref[0, pr * LANE:(pr + 1) * LANE, :] = decay * st + st_new
        ys.append(y_diag + y_off + jnp.where(lo, dsk[0], dsk[1]) * xp)
    y = jnp.concatenate(ys, axis=1)
    y_ref[...] = _gated_norm(y, z_ref[...], gn_ref[...])


def _ssd_chunk(xbc, z, dt, w, *, batch, seq):
    chunk = SSD_CHUNK if seq % SSD_CHUNK == 0 else seq
    nc = seq // chunk
    t = batch * seq
    row = lambda b, c: (b * nc + c, 0)
    return pl.pallas_call(
        functools.partial(_ssd_chunk_kernel, chunk=chunk),
        grid=(batch, nc),
        in_specs=[pl.BlockSpec((chunk, CONV_CH), row), pl.BlockSpec((chunk, SSM_INNER), row),
                  pl.BlockSpec((chunk, LANE), row), _full((CONV_K, CONV_CH)), _full((1, CONV_CH)), _full((1, LANE)),
                  _full((1, LANE)), _full((1, LANE)), _full((1, SSM_INNER))],
        out_specs=[pl.BlockSpec((chunk, SSM_INNER), row),
                   pl.BlockSpec((1, SSM_INNER, D_STATE), lambda b, c: (b, 0, 0))],
        out_shape=[jax.ShapeDtypeStruct((t, SSM_INNER), F32), jax.ShapeDtypeStruct((batch, SSM_INNER, D_STATE), F32)],
        scratch_shapes=[pltpu.VMEM((8 + chunk, CONV_CH), F32)],
        compiler_params=_cparams(("parallel", "arbitrary")), name="ssd_chunk",
    )(xbc, z, dt, w['conv_w'], w['conv_b'], w['dt_bias_p'], w['a_log_p'], w['d_skip_p'], w['norm_ssm_out'])


def _ssd_step_kernel(xbc_ref, cprev_ref, z_ref, dt_ref, st_ref, cw_ref, cb_ref, dtb_ref, alog_ref, dskip_ref, gn_ref,
                     y_ref, cnew_ref, stn_ref, *, nb):
    xraw = xbc_ref[...]
    conv = cb_ref[...] + xraw * cw_ref[CONV_K - 1:CONV_K, :]
    for kk in range(CONV_K - 1):
        conv = conv + cprev_ref[kk] * cw_ref[kk:kk + 1, :]
    for kk in range(CONV_K - 2):
        cnew_ref[kk] = cprev_ref[kk + 1]
    cnew_ref[CONV_K - 2] = xraw
    xbc = _silu(conv)
    xs = xbc[:, :SSM_INNER]
    dt = _softplus(dt_ref[...] + dtb_ref[...])
    decay = jnp.exp(dt * (-jnp.exp(alog_ref[...])))
    tok_l = lax.broadcasted_iota(jnp.int32, (1, nb), 1)
    heads_per_group = SSM_HEADS // SSM_GROUPS
    xs_t = xs.T
    ys = []
    for hd in range(SSM_HEADS):
        g = hd // heads_per_group
        bm = xbc[:, SSM_INNER + g * D_STATE:SSM_INNER + (g + 1) * D_STATE]
        cm = xbc[:, SSM_INNER + (SSM_GROUPS + g) * D_STATE:SSM_INNER + (SSM_GROUPS + g + 1) * D_STATE].astype(BF16)
        x_t = xs_t[hd * SSM_HEADDIM:(hd + 1) * SSM_HEADDIM, :]
        dtb = bm * dt[:, hd:hd + 1]
        y_t = jnp.zeros((SSM_HEADDIM, nb), F32)
        for i in range(nb):
            new = decay[i:i + 1, hd:hd + 1] * st_ref[i, hd] + x_t[:, i:i + 1] * dtb[i:i + 1, :]
            stn_ref[i, hd] = new
            yi = _dot_nt(new.astype(BF16), cm)
            y_t = y_t + jnp.where(tok_l == i, yi, 0.0)
        ys.append(y_t)
    y = jnp.concatenate(ys, axis=0).T
    dsk = jnp.concatenate([jnp.broadcast_to(dskip_ref[:, hd:hd + 1], (1, SSM_HEADDIM)) for hd in range(SSM_HEADS)],
                          axis=1)
    y = y + dsk * xs
    y_ref[...] = _gated_norm(y, z_ref[...], gn_ref[...])


def _ssd_step(xbc, conv_prev, z, dt, state, w, *, nb):
    t = xbc.shape[0]
    row = lambda i: (i, 0)
    return pl.pallas_call(
        functools.partial(_ssd_step_kernel, nb=nb),
        grid=(t // nb,),
        in_specs=[pl.BlockSpec((nb, CONV_CH), row), pl.BlockSpec((CONV_K - 1, nb, CONV_CH), lambda i: (0, i, 0)),
                  pl.BlockSpec((nb, SSM_INNER), row), pl.BlockSpec((nb, LANE), row),
                  pl.BlockSpec((nb, SSM_HEADS, SSM_HEADDIM, D_STATE), lambda i: (i, 0, 0, 0)),
                  _full((CONV_K, CONV_CH)), _full((1, CONV_CH)), _full((1, LANE)), _full((1, LANE)), _full((1, LANE)),
                  _full((1, SSM_INNER))],
        out_specs=[pl.BlockSpec((nb, SSM_INNER), row), pl.BlockSpec((CONV_K - 1, nb, CONV_CH), lambda i: (0, i, 0)),
                   pl.BlockSpec((nb, SSM_HEADS, SSM_HEADDIM, D_STATE), lambda i: (i, 0, 0, 0))],
        out_shape=[jax.ShapeDtypeStruct((t, SSM_INNER), F32), jax.ShapeDtypeStruct((CONV_K - 1, t, CONV_CH), F32),
                   jax.ShapeDtypeStruct((t, SSM_HEADS, SSM_HEADDIM, D_STATE), F32)],
        compiler_params=_cparams(("parallel",)), name="ssd_step",
    )(xbc, conv_prev, z, dt, state, w['conv_w'], w['conv_b'], w['dt_bias_p'], w['a_log_p'], w['d_skip_p'],
      w['norm_ssm_out'])


def _out_proj_kernel(*refs, absorbed):
    if absorbed:
        (olat_ref, wuv_ref, yn_ref, x_ref, ga_ref, wout_ref, gffn_ref, wr_ref, br_ref, x1_ref, h_ref, route_ref) = refs
        o_attn = _dot(olat_ref[...].astype(BF16), wuv_ref[...])
    else:
        (oat_ref, yn_ref, x_ref, ga_ref, wout_ref, gffn_ref, wr_ref, br_ref, x1_ref, h_ref, route_ref) = refs
        o_attn = oat_ref[...]
    n_attn = MLA_HEADS * V_DIM
    oa = _rms(o_attn, ga_ref[...]).astype(BF16)
    mix = _dot(oa, wout_ref[0:n_attn, :]) + _dot(yn_ref[...].astype(BF16), wout_ref[n_attn:, :])
    x1 = x_ref[...] + mix
    x1_ref[...] = x1
    h = _rms(x1, gffn_ref[...])
    h_ref[...] = h
    logits = jnp.dot(h, wr_ref[...], precision=lax.Precision.HIGHEST, preferred_element_type=F32) + br_ref[...]
    lane = lax.broadcasted_iota(jnp.int32, logits.shape, 1)
    neg = -jnp.inf
    gl = jnp.where(lane < N_GROUPS, logits, neg)
    gmax = jnp.max(gl, axis=-1, keepdims=True)
    g_w = 1.0 / jnp.sum(jnp.exp(gl - gmax), axis=-1, keepdims=True)
    g_idx = jnp.min(jnp.where(gl == gmax, lane, LANE), axis=-1, keepdims=True)
    e_lo = N_GROUPS + EXPERTS_PER_GROUP * g_idx
    el = jnp.where((lane >= e_lo) & (lane < e_lo + EXPERTS_PER_GROUP), logits, neg)
    e1 = jnp.max(el, axis=-1, keepdims=True)
    i1 = jnp.min(jnp.where(el == e1, lane, LANE), axis=-1, keepdims=True)
    el2 = jnp.where(lane == i1, neg, el)
    e2 = jnp.max(el2, axis=-1, keepdims=True)
    i2 = jnp.min(jnp.where(el2 == e2, lane, LANE), axis=-1, keepdims=True)
    r = jnp.exp(e2 - e1)
    c1 = g_w / (1.0 + r)
    c2 = g_w * r / (1.0 + r)
    route = jnp.where(lane == 0, (i1 - N_GROUPS).astype(F32), 0.0)
    route = jnp.where(lane == 1, (i2 - N_GROUPS).astype(F32), route)
    route = jnp.where(lane == 2, c1, route)
    route = jnp.where(lane == 3, c2, route)
    route_ref[...] = route


def _out_proj(o_in, yn, x, w, *, absorbed, tm):
    t, d = x.shape
    row = lambda i: (i, 0)
    n_mix = MLA_HEADS * V_DIM + SSM_INNER
    if absorbed:
        first = [pl.BlockSpec((tm, MLA_HEADS * KV_LORA), row), _full((MLA_HEADS * KV_LORA, MLA_HEADS * V_DIM))]
        args = [o_in, w['w_uv_bd']]
    else:
        first = [pl.BlockSpec((tm, MLA_HEADS * V_DIM), row)]
        args = [o_in]
    in_specs = first + [pl.BlockSpec((tm, SSM_INNER), row), pl.BlockSpec((tm, d), row), _full((1, MLA_HEADS * V_DIM)),
                        _full((n_mix, d)), _full((1, d)), _full((d, LANE)), _full((1, LANE))]
    args += [yn, x, w['norm_attn_out'], w['w_out'], w['norm_ffn'], w['w_router'], w['b_router']]
    return pl.pallas_call(
        functools.partial(_out_proj_kernel, absorbed=absorbed),
        grid=(t // tm,), in_specs=in_specs,
        out_specs=[pl.BlockSpec((tm, d), row), pl.BlockSpec((tm, d), row), pl.BlockSpec((tm, LANE), row)],
        out_shape=[jax.ShapeDtypeStruct((t, d), F32), jax.ShapeDtypeStruct((t, d), F32),
                   jax.ShapeDtypeStruct((t, LANE), F32)],
        compiler_params=_cparams(("parallel",)), name="out_proj_abs" if absorbed else "out_proj",
    )(*args)


def _row_gather(idx_ref, src_hbm, dst, sem, n_rows):
    def body(r, carry):
        t = idx_ref[0, 0, r]
        pltpu.make_async_copy(src_hbm.at[pl.ds(t, 1), :], dst.at[pl.ds(r, 1), :], sem).start()
        return carry
    lax.fori_loop(0, n_rows, body, 0)


def _row_gather_wait(src_hbm, dst, sem, n_rows):
    def body(r, carry):
        pltpu.make_async_copy(src_hbm.at[pl.ds(0, 1), :], dst.at[pl.ds(r, 1), :], sem).wait()
        return carry
    lax.fori_loop(0, n_rows, body, 0)


def _moe_kernel(te_ref, tok_ref, tokn_ref, h_hbm, wg_ref, wu_ref, wd_ref, y_ref, xbuf, sem, *, tm):
    i = pl.program_id(0)
    n = pl.num_programs(0)
    slot = i % 2

    @pl.when(i == 0)
    def _():
        _row_gather(tok_ref, h_hbm, xbuf.at[0], sem.at[0], tm)

    @pl.when(i + 1 < n)
    def _():
        _row_gather(tokn_ref, h_hbm, xbuf.at[1 - slot], sem.at[1 - slot], tm)

    _row_gather_wait(h_hbm, xbuf.at[slot], sem.at[slot], tm)
    x = xbuf[slot].astype(BF16)
    act = _silu(_dot(x, wg_ref[0])) * _dot(x, wu_ref[0])
    y_ref[...] = _dot(act.astype(BF16), wd_ref[0])
    del te_ref


def _moe(h, tile_expert, row_token, w, *, tm):
    t, d = h.shape
    n_tiles = tile_expert.shape[0]
    grid_spec = pltpu.PrefetchScalarGridSpec(
        num_scalar_prefetch=1, grid=(n_tiles,),
        in_specs=[pl.BlockSpec((1, 1, tm), lambda i, te: (i, 0, 0), memory_space=pltpu.SMEM),
                  pl.BlockSpec((1, 1, tm), lambda i, te: (jnp.minimum(i + 1, n_tiles - 1), 0, 0),
                               memory_space=pltpu.SMEM),
                  pl.BlockSpec(memory_space=pl.ANY),
                  pl.BlockSpec((1, d, EXPERT_FF), lambda i, te: (te[i], 0, 0)),
                  pl.BlockSpec((1, d, EXPERT_FF), lambda i, te: (te[i], 0, 0)),
                  pl.BlockSpec((1, EXPERT_FF, d), lambda i, te: (te[i], 0, 0))],
        out_specs=pl.BlockSpec((tm, d), lambda i, te: (i, 0)),
        scratch_shapes=[pltpu.VMEM((2, tm, d), F32), pltpu.SemaphoreType.DMA((2,))])
    rt = row_token.reshape(n_tiles, 1, tm)
    return pl.pallas_call(
        functools.partial(_moe_kernel, tm=tm), grid_spec=grid_spec,
        out_shape=jax.ShapeDtypeStruct((n_tiles * tm, d), F32),
        compiler_params=_cparams(("arbitrary",)), name="moe",
    )(tile_expert, rt, rt, h, w['w_exp_gate'], w['w_exp_up'], w['w_exp_down'])


def _moe_plan(route, *, tm):
    t = route.shape[0]
    eids = route[:, 0:2].astype(jnp.int32).reshape(-1)
    onehot = (eids[:, None] == jnp.arange(N_EXPERTS, dtype=jnp.int32)[None, :]).astype(jnp.int32)
    csum = jnp.cumsum(onehot, axis=0)
    rank = jnp.sum(onehot * (csum - 1), axis=1)
    counts = csum[-1]
    padded = ((counts + tm - 1) // tm) * tm
    ends = jnp.cumsum(padded)
    offs = ends - padded
    pos = offs[eids] + rank
    n_tiles = (2 * t) // tm + N_EXPERTS
    row_token = jnp.zeros((n_tiles * tm,), jnp.int32).at[pos].set(jnp.arange(2 * t, dtype=jnp.int32) // 2)
    tile_start = jnp.arange(n_tiles, dtype=jnp.int32) * tm
    tile_expert = jnp.minimum(jnp.searchsorted(ends, tile_start, side='right'), N_EXPERTS - 1).astype(jnp.int32)
    return tile_expert, row_token, pos.reshape(t, 2)


def _final_kernel(pos_ref, posn_ref, x1_ref, route_ref, p_ref, y_hbm, gple_ref, wgate_ref, wproj_ref, gfin_ref,
                  out_ref, ybuf, sem, *, tm):
    i = pl.program_id(0)
    n = pl.num_programs(0)
    slot = i % 2

    @pl.when(i == 0)
    def _():
        _row_gather(pos_ref, y_hbm, ybuf.at[0], sem.at[0], 2 * tm)

    @pl.when(i + 1 < n)
    def _():
        _row_gather(posn_ref, y_hbm, ybuf.at[1 - slot], sem.at[1 - slot], 2 * tm)

    _row_gather_wait(y_hbm, ybuf.at[slot], sem.at[slot], 2 * tm)
    route = route_ref[...]
    c1 = route[:, 2:3]
    c2 = route[:, 3:4]
    x2 = x1_ref[...] + (c1 * ybuf[slot, 0:tm, :] + c2 * ybuf[slot, tm:2 * tm, :])
    gate_in = _rms(x2, gple_ref[...]).astype(BF16)
    gate = 1.0 / (1.0 + jnp.exp(-_dot(gate_in, wgate_ref[...])))
    x3 = x2 + gate * _dot(p_ref[...].astype(BF16), wproj_ref[...])
    out_ref[...] = _rms(x3, gfin_ref[...])


def _final(x1, route, p, y_sorted, pos, w, *, tm):
    t, d = x1.shape
    nt = t // tm
    ple = p.shape[1]
    pos_t = pos.reshape(nt, tm, 2).transpose(0, 2, 1).reshape(nt, 1, 2 * tm)
    row = lambda i: (i, 0)
    in_specs = [pl.BlockSpec((1, 1, 2 * tm), lambda i: (i, 0, 0), memory_space=pltpu.SMEM),
                pl.BlockSpec((1, 1, 2 * tm), lambda i: (jnp.minimum(i + 1, nt - 1), 0, 0), memory_space=pltpu.SMEM),
                pl.BlockSpec((tm, d), row), pl.BlockSpec((tm, LANE), row), pl.BlockSpec((tm, ple), row),
                pl.BlockSpec(memory_space=pl.ANY), _full((1, d)), _full((d, d)), _full((ple, d)), _full((1, d))]
    return pl.pallas_call(
        functools.partial(_final_kernel, tm=tm), grid=(nt,), in_specs=in_specs,
        out_specs=pl.BlockSpec((tm, d), row), out_shape=jax.ShapeDtypeStruct((t, d), F32),
        scratch_shapes=[pltpu.VMEM((2, 2 * tm, d), F32), pltpu.SemaphoreType.DMA((2,))],
        compiler_params=_cparams(("arbitrary",)), name="final",
    )(pos_t, pos_t, x1, route, p, y_sorted, w['norm_ple'], w['w_ple_gate'], w['w_ple_proj'], w['norm_final'])


def _pad_lanes(v, width=LANE):
    v = v.reshape(1, -1)
    return jnp.pad(v, ((0, 0), (0, width - v.shape[1])))


def _rot_cols(wr):
    half = QK_ROPE // 2
    return jnp.concatenate([-wr[..., half:], wr[..., :half]], axis=-1)


def _prep_weights(norm_mix, w_in, norm_q, w_uq, norm_kv, w_ukv, norm_attn_out, conv_w, conv_b, dt_bias, a_log, d_skip,
                  norm_ssm_out, w_out, norm_ffn, w_group_router, b_group_router, w_expert_router, b_expert_router,
                  w_exp_gate, w_exp_up, w_exp_down, norm_ple, w_ple_gate, w_ple_proj, norm_final):
    d = w_in.shape[0]
    splits = [Q_LORA, KV_LORA, QK_ROPE, SSM_INNER, CONV_CH, SSM_HEADS]
    bounds = [0]
    for s in splits:
        bounds.append(bounds[-1] + s)
    w_cq, w_ckv, w_kr, w_z, w_xbc, w_dt = [w_in[:, bounds[j]:bounds[j + 1]] for j in range(6)]
    zpad = lambda n: jnp.zeros((d, n), F32)
    place = lambda m: jnp.concatenate([zpad(ROPE_LANE0), m, zpad(LANE - ROPE_LANE0 - QK_ROPE)], axis=1)
    w_in_p = jnp.concatenate([w_cq, w_ckv, place(w_kr), place(_rot_cols(w_kr)), w_z, w_xbc, w_dt,
                              zpad(LANE - SSM_HEADS)], axis=1).astype(BF16)
    nope, rp = w_uq[..., :QK_NOPE], w_uq[..., QK_NOPE:]
    zq = lambda n: jnp.zeros((Q_LORA, MLA_HEADS, n), F32)
    tail = HEAD_PAD - QK_NOPE - QK_ROPE
    wq_a = jnp.concatenate([nope, rp, zq(tail)], axis=-1).reshape(Q_LORA, -1).astype(BF16)
    wq_b = jnp.concatenate([zq(QK_NOPE), _rot_cols(rp), zq(tail)], axis=-1).reshape(Q_LORA, -1).astype(BF16)
    w_uk, w_uv = w_ukv[..., :QK_NOPE], w_ukv[..., QK_NOPE:]
    wk_p = jnp.concatenate([w_uk, jnp.zeros((KV_LORA, MLA_HEADS, HEAD_PAD - QK_NOPE), F32)], axis=-1)
    w_kv_p = jnp.concatenate([wk_p.reshape(KV_LORA, -1), w_uv.reshape(KV_LORA, -1)], axis=1).astype(BF16)
    w_abs = jnp.zeros((MLA_HEADS, HEAD_PAD, Q_ABS), F32)
    w_abs = w_abs.at[:, :QK_NOPE, :KV_LORA].set(jnp.transpose(w_uk, (1, 2, 0)))
    w_abs = w_abs.at[:, QK_NOPE:QK_NOPE + QK_ROPE, KV_LORA:KV_LORA + QK_ROPE].set(
        jnp.broadcast_to(jnp.eye(QK_ROPE, dtype=F32), (MLA_HEADS, QK_ROPE, QK_ROPE)))
    w_uv_bd = jnp.zeros((MLA_HEADS, KV_LORA, MLA_HEADS, V_DIM), F32)
    hidx = jnp.arange(MLA_HEADS)
    w_uv_bd = w_uv_bd.at[hidx, :, hidx, :].set(jnp.transpose(w_uv, (1, 0, 2)))
    w_router = jnp.concatenate([w_group_router, w_expert_router,
                                jnp.zeros((d, LANE - N_GROUPS - N_EXPERTS), F32)], axis=1)
    b_router = _pad_lanes(jnp.concatenate([b_group_router, b_expert_router]))
    return dict(
        norm_mix=norm_mix.reshape(1, -1), w_in_p=w_in_p, norm_q=norm_q.reshape(1, -1), wq_a=wq_a, wq_b=wq_b,
        norm_kv=norm_kv.reshape(1, -1), w_kv_p=w_kv_p, w_abs=w_abs.astype(BF16),
        w_uv_bd=w_uv_bd.reshape(MLA_HEADS * KV_LORA, MLA_HEADS * V_DIM).astype(BF16),
        norm_attn_out=norm_attn_out.reshape(1, -1), conv_w=conv_w, conv_b=conv_b.reshape(1, -1),
        dt_bias_p=_pad_lanes(dt_bias), a_log_p=_pad_lanes(a_log), d_skip_p=_pad_lanes(d_skip),
        norm_ssm_out=norm_ssm_out.reshape(1, -1), w_out=w_out.astype(BF16), norm_ffn=norm_ffn.reshape(1, -1),
        w_router=w_router, b_router=b_router, w_exp_gate=w_exp_gate.astype(BF16), w_exp_up=w_exp_up.astype(BF16),
        w_exp_down=w_exp_down.astype(BF16), norm_ple=norm_ple.reshape(1, -1), w_ple_gate=w_ple_gate.astype(BF16),
        w_ple_proj=w_ple_proj.astype(BF16), norm_final=norm_final.reshape(1, -1))


def _rope_tables(pos):
    half = QK_ROPE // 2
    inv = ROPE_THETA ** (-jnp.arange(half, dtype=F32) / half)
    ang = pos.astype(F32)[:, None] * inv[None, :]
    place = lambda m: jnp.pad(jnp.concatenate([m, m], axis=1), ((0, 0), (ROPE_LANE0, LANE - ROPE_LANE0 - QK_ROPE)))
    return place(jnp.cos(ang)), place(jnp.sin(ang))


def _tile(n, pref):
    return pref if n % pref == 0 else n


def _ffn_tail(x1, h, route, p, w, *, tm_moe, tm_fin):
    tile_expert, row_token, pos = _moe_plan(route, tm=tm_moe)
    y_sorted = _moe(h, tile_expert, row_token, w, tm=tm_moe)
    return _final(x1, route, p, y_sorted, pos, w, tm=tm_fin)


def kernel(x_prompt, x_sample, p_prompt, p_sample, cache_ckv, cache_krope, state_conv, state_ssm, page_table, norm_mix, w_in, norm_q, w_uq, norm_kv, w_ukv, norm_attn_out, conv_w, conv_b, dt_bias, a_log, d_skip, norm_ssm_out, w_out, norm_ffn, w_group_router, b_group_router, w_expert_router, b_expert_router, w_exp_gate, w_exp_up, w_exp_down, norm_ple, w_ple_gate, w_ple_proj, norm_final):
    depth = w_in.shape[0]
    assert depth == 1, "single-layer stack"
    bp, sp, d = x_prompt.shape
    bs, ls, _ = x_sample.shape
    assert ls == 1, "decode step handles one new token per sequence"
    n_pool, page, _ = cache_ckv.shape[1:]
    n_pages = page_table.shape[1]
    past_len = n_pages * page
    w = _prep_weights(norm_mix[0], w_in[0], norm_q[0], w_uq[0], norm_kv[0], w_ukv[0], norm_attn_out[0], conv_w[0],
                      conv_b[0], dt_bias[0], a_log[0], d_skip[0], norm_ssm_out[0], w_out[0], norm_ffn[0],
                      w_group_router[0], b_group_router[0], w_expert_router[0], b_expert_router[0], w_exp_gate[0],
                      w_exp_up[0], w_exp_down[0], norm_ple[0], w_ple_gate[0], w_ple_proj[0], norm_final)

    tp = bp * sp
    xp = x_prompt.reshape(tp, d)
    tm_p = _tile(sp, 512)
    cos_p, sin_p = _rope_tables(jnp.arange(sp))
    q, k, v, ckv_p, krp_p, z_p, xbc_p, dt_p = _in_proj(xp, cos_p, sin_p, w, absorb=False, tm=tm_p,
                                                       n_pos_blocks=sp // tm_p)
    tq = _tile(sp, 256)
    o_attn = _flash(q, k, v, batch=bp, seq=sp, tq=tq, tk=tq)
    yn_p, ssm_p = _ssd_chunk(xbc_p, z_p, dt_p, w, batch=bp, seq=sp)
    x1_p, h_p, route_p = _out_proj(o_attn, yn_p, xp, w, absorbed=False, tm=_tile(tp, 512))
    y_prompt = _ffn_tail(x1_p, h_p, route_p, p_prompt[0].reshape(tp, -1), w, tm_moe=_tile(2 * tp, 256),
                         tm_fin=_tile(tp, 256))

    xs = x_sample.reshape(bs, d)
    cos_s, sin_s = _rope_tables(jnp.full((bs,), past_len, jnp.int32))
    q_abs, ckv_s, krp_s, z_s, xbc_s, dt_s = _in_proj(xs, cos_s, sin_s, w, absorb=True, tm=bs, n_pos_blocks=1)
    ch = 16 if n_pages % 16 == 0 else n_pages
    o_lat = _paged(q_abs.reshape(bs, MLA_HEADS, Q_ABS), ckv_s.reshape(bs, 1, KV_LORA), krp_s.reshape(bs, 1, LANE),
                   cache_ckv.reshape(n_pool, page, KV_LORA), cache_krope.reshape(n_pool, page, QK_ROPE), page_table,
                   ch=ch, nslot=3)
    conv_prev = jnp.transpose(state_conv[0], (1, 0, 2))
    yn_s, conv_new, ssm_s = _ssd_step(xbc_s, conv_prev, z_s, dt_s, state_ssm[0], w, nb=_tile(bs, 8))
    x1_s, h_s, route_s = _out_proj(o_lat.reshape(bs, MLA_HEADS * KV_LORA), yn_s, xs, w, absorbed=True, tm=bs)
    y_sample = _ffn_tail(x1_s, h_s, route_s, p_sample[0].reshape(bs, -1), w, tm_moe=_tile(2 * bs, 128),
                         tm_fin=bs)

    rope_sl = slice(ROPE_LANE0, ROPE_LANE0 + QK_ROPE)
    return (y_prompt.reshape(bp, sp, d), y_sample.reshape(bs, ls, d),
            ckv_p.reshape(1, bp, sp, KV_LORA), krp_p[:, rope_sl].reshape(1, bp, sp, QK_ROPE),
            xbc_p.reshape(bp, sp, CONV_CH)[:, sp - (CONV_K - 1):, :][None],
            ssm_p.reshape(1, bp, SSM_HEADS, SSM_HEADDIM, D_STATE),
            ckv_s.reshape(1, bs, ls, KV_LORA), krp_s[:, rope_sl].reshape(1, bs, ls, QK_ROPE),
            jnp.transpose(conv_new, (1, 0, 2))[None], ssm_s[None])
```

```python
import functools
import math

import jax
import jax.numpy as jnp
from jax import lax
from jax.experimental import pallas as pl
from jax.experimental.pallas import tpu as pltpu

F32 = jnp.float32
BF16 = jnp.bfloat16

MLA_HEADS = 8
QK_NOPE = 64
QK_ROPE = 32
V_DIM = 64
Q_LORA = 384
KV_LORA = 256
ROPE_THETA = 10000.0
ATTN_SCALE = 1.0 / math.sqrt(QK_NOPE + QK_ROPE)
LOG2E = math.log2(math.e)
Q_SCALE = ATTN_SCALE * LOG2E
SSM_HEADS = 8
SSM_HEADDIM = 64
SSM_INNER = SSM_HEADS * SSM_HEADDIM
SSM_GROUPS = 2
D_STATE = 128
CONV_K = 4
CONV_CH = SSM_INNER + 2 * SSM_GROUPS * D_STATE
SSD_CHUNK = 128
N_GROUPS = 4
EXPERTS_PER_GROUP = 4
N_EXPERTS = N_GROUPS * EXPERTS_PER_GROUP
EXPERT_FF = 512
EPS = 1e-6

LANE = 128
HEAD_PAD = 128
ROPE_LANE0 = QK_NOPE
Q_ABS = KV_LORA + LANE

C_CQ = 0
C_CKV = C_CQ + Q_LORA
C_KRA = C_CKV + KV_LORA
C_KRB = C_KRA + LANE
C_Z = C_KRB + LANE
C_XBC = C_Z + SSM_INNER
C_DT = C_XBC + CONV_CH
IN_COLS_P = C_DT + LANE

VMEM_LIMIT = 56 * 1024 * 1024


def _cparams(sem, vmem=VMEM_LIMIT):
    return pltpu.CompilerParams(dimension_semantics=sem, vmem_limit_bytes=vmem)


def _rms(x, g):
    return x * lax.rsqrt(jnp.mean(x * x, axis=-1, keepdims=True) + EPS) * g


def _dot(a, b):
    return jnp.dot(a, b, preferred_element_type=F32)


def _mm(a, w):
    if w.dtype == BF16:
        return jnp.dot(a.astype(BF16), w, preferred_element_type=F32)
    return jnp.dot(a, w, precision=lax.Precision.HIGHEST, preferred_element_type=F32)


def _dot_nt(a, b):
    return lax.dot_general(a, b, (((1,), (1,)), ((), ())), preferred_element_type=F32)


def _dot_tn(a, b):
    return lax.dot_general(a, b, (((0,), (0,)), ((), ())), preferred_element_type=F32)


def _silu(x):
    return x * (1.0 / (1.0 + jnp.exp(-x)))


def _softplus(x):
    return jnp.maximum(x, 0.0) + jnp.log(1.0 + jnp.exp(-jnp.abs(x)))


def _full(shape):
    nd = len(shape)
    return pl.BlockSpec(shape, lambda *a: (0,) * nd)


def _in_proj_kernel(x_ref, gmix_ref, win_ref, gq_ref, wqa_ref, wqb_ref, gkv_ref, wkv_ref, cos_ref, sin_ref,
                    *refs, absorb):
    if absorb:
        wabs_ref, q_ref, ckv_ref, krp_ref, z_ref, xbc_ref, dt_ref = refs
    else:
        q_ref, k_ref, v_ref, ckv_ref, krp_ref, z_ref, xbc_ref, dt_ref = refs
    h = _rms(x_ref[...], gmix_ref[...])
    u = _mm(h, win_ref[...])
    cq = _rms(u[:, C_CQ:C_CQ + Q_LORA], gq_ref[...])
    qa = _mm(cq, wqa_ref[...])
    qb = _mm(cq, wqb_ref[...])
    cos = cos_ref[...]
    sin = sin_ref[...]
    lane = lax.broadcasted_iota(jnp.int32, (1, LANE), 1)
    q_cos = Q_SCALE * (jnp.where(lane < QK_NOPE, 1.0, 0.0) + cos)
    q_sin = Q_SCALE * sin
    ckv = _rms(u[:, C_CKV:C_CKV + KV_LORA], gkv_ref[...])
    ckv_ref[...] = ckv
    krp = u[:, C_KRA:C_KRA + LANE] * cos + u[:, C_KRB:C_KRB + LANE] * sin
    krp_ref[...] = krp
    z_ref[...] = u[:, C_Z:C_Z + SSM_INNER]
    xbc_ref[...] = u[:, C_XBC:C_XBC + CONV_CH]
    dt_ref[...] = u[:, C_DT:C_DT + LANE]
    for hd in range(MLA_HEADS):
        sl = slice(hd * HEAD_PAD, (hd + 1) * HEAD_PAD)
        qh = qa[:, sl] * q_cos + qb[:, sl] * q_sin
        if absorb:
            q_ref[:, hd * Q_ABS:(hd + 1) * Q_ABS] = _mm(qh, wabs_ref[hd]).astype(BF16)
        else:
            q_ref[:, sl] = qh.astype(BF16)
    if not absorb:
        kv = _mm(ckv, wkv_ref[...])
        for hd in range(MLA_HEADS):
            sl = slice(hd * HEAD_PAD, (hd + 1) * HEAD_PAD)
            k_ref[:, sl] = (kv[:, sl] + krp).astype(BF16)
        v_ref[...] = kv[:, MLA_HEADS * HEAD_PAD:].astype(BF16)


def _in_proj(x, cos, sin, w, *, absorb, tm, n_pos_blocks):
    t, d = x.shape
    nt = t // tm
    row = lambda i: (i, 0)
    pos = lambda i: (i % n_pos_blocks, 0)
    in_specs = [
        pl.BlockSpec((tm, d), row), _full((1, d)), _full((d, IN_COLS_P)), _full((1, Q_LORA)),
        _full((Q_LORA, MLA_HEADS * HEAD_PAD)), _full((Q_LORA, MLA_HEADS * HEAD_PAD)), _full((1, KV_LORA)),
        _full((KV_LORA, MLA_HEADS * (HEAD_PAD + V_DIM))), pl.BlockSpec((tm, LANE), pos), pl.BlockSpec((tm, LANE), pos),
    ]
    sfx = '32' if absorb else ''
    args = [x, w['norm_mix'], w['w_in_p' + sfx], w['norm_q'], w['wq_a' + sfx], w['wq_b' + sfx], w['norm_kv'],
            w['w_kv_p'], cos, sin]
    tail_shapes = [
        jax.ShapeDtypeStruct((t, KV_LORA), F32), jax.ShapeDtypeStruct((t, LANE), F32),
        jax.ShapeDtypeStruct((t, SSM_INNER), F32), jax.ShapeDtypeStruct((t, CONV_CH), F32),
        jax.ShapeDtypeStruct((t, LANE), F32),
    ]
    tail_specs = [pl.BlockSpec((tm, KV_LORA), row), pl.BlockSpec((tm, LANE), row), pl.BlockSpec((tm, SSM_INNER), row),
                  pl.BlockSpec((tm, CONV_CH), row), pl.BlockSpec((tm, LANE), row)]
    if absorb:
        in_specs.append(_full((MLA_HEADS, HEAD_PAD, Q_ABS)))
        args.append(w['w_abs32'])
        out_shape = [jax.ShapeDtypeStruct((t, MLA_HEADS * Q_ABS), BF16)] + tail_shapes
        out_specs = [pl.BlockSpec((tm, MLA_HEADS * Q_ABS), row)] + tail_specs
    else:
        out_shape = [jax.ShapeDtypeStruct((t, MLA_HEADS * HEAD_PAD), BF16),
                     jax.ShapeDtypeStruct((t, MLA_HEADS * HEAD_PAD), BF16),
                     jax.ShapeDtypeStruct((t, MLA_HEADS * V_DIM), BF16)] + tail_shapes
        out_specs = [pl.BlockSpec((tm, MLA_HEADS * HEAD_PAD), row), pl.BlockSpec((tm, MLA_HEADS * HEAD_PAD), row),
                     pl.BlockSpec((tm, MLA_HEADS * V_DIM), row)] + tail_specs
    return pl.pallas_call(
        functools.partial(_in_proj_kernel, absorb=absorb),
        grid=(nt,), in_specs=in_specs, out_specs=out_specs, out_shape=out_shape,
        compiler_params=_cparams(("parallel",)), name="in_proj_abs" if absorb else "in_proj",
    )(*args)


def _flash_kernel(q_ref, k_ref, v_ref, o_ref, *, tq, tk):
    qi = pl.program_id(2)
    lane = lax.broadcasted_iota(jnp.int32, (1, LANE), 1)
    qs = [q_ref[:, hh * HEAD_PAD:(hh + 1) * HEAD_PAD] for hh in range(2)]

    def step(ki, carry, masked):
        ks = pl.multiple_of(ki * tk, tk)
        v = v_ref[pl.ds(ks, tk), :]
        if masked:
            row = qi * tq + lax.broadcasted_iota(jnp.int32, (tq, tk), 0)
            col = ks + lax.broadcasted_iota(jnp.int32, (tq, tk), 1)
            keep = col <= row
        out = []
        for hh in range(2):
            m, l, acc = carry[hh]
            s = _dot_nt(qs[hh], k_ref[pl.ds(ks, tk), hh * HEAD_PAD:(hh + 1) * HEAD_PAD])
            if masked:
                s = jnp.where(keep, s, -jnp.inf)
            m_new = jnp.maximum(m, jnp.max(s, axis=-1, keepdims=True))
            alpha = jnp.exp2(m - m_new)
            p = jnp.exp2(s - m_new)
            l = alpha * l + jnp.sum(p, axis=-1, keepdims=True)
            acc = alpha * acc + _dot(p.astype(BF16), v)
            out.append((m_new, l, acc))
        return tuple(out)

    init1 = (jnp.full((tq, 1), -jnp.inf, F32), jnp.zeros((tq, 1), F32), jnp.zeros((tq, 2 * V_DIM), F32))
    n_full = (qi * tq) // tk
    carry = lax.fori_loop(0, n_full, functools.partial(step, masked=False), (init1, init1))
    n_masked = (tq + tk - 1) // tk
    for d in range(n_masked):
        carry = step(n_full + d, carry, True)
    o0 = carry[0][2] / carry[0][1]
    o1 = carry[1][2] / carry[1][1]
    o_ref[...] = jnp.where(lane < V_DIM, o0, o1)


def _flash(q, k, v, *, batch, seq, tq, tk):
    t = batch * seq
    nq = seq // tq
    npair = MLA_HEADS // 2
    return pl.pallas_call(
        functools.partial(_flash_kernel, tq=tq, tk=tk),
        grid=(batch, npair, nq),
        in_specs=[pl.BlockSpec((tq, 2 * HEAD_PAD), lambda b, hp, i: (b * nq + i, hp)),
                  pl.BlockSpec((seq, 2 * HEAD_PAD), lambda b, hp, i: (b, hp)),
                  pl.BlockSpec((seq, 2 * V_DIM), lambda b, hp, i: (b, hp))],
        out_specs=pl.BlockSpec((tq, 2 * V_DIM), lambda b, hp, i: (b * nq + i, hp)),
        out_shape=jax.ShapeDtypeStruct((t, MLA_HEADS * V_DIM), F32),
        compiler_params=_cparams(("parallel", "parallel", "arbitrary")), name="flash",
    )(q, k, v)


def _paged_kernel(pt_ref, q_ref, ckvn_ref, krn_ref, cc_hbm, ck_hbm, o_ref, cbuf, kbuf, sem, m_sc, l_sc, acc_sc,
                  *, n_chunks, ch, page, nslot, n_batch):
    b = pl.program_id(0)
    total = n_batch * n_chunks

    def copies(g, slot):
        gb = g // n_chunks
        gc = g % n_chunks
        out = []
        for j in range(ch):
            pg = pt_ref[gb, gc * ch + j]
            out.append(pltpu.make_async_copy(cc_hbm.at[pg], cbuf.at[slot, pl.ds(j * page, page), :], sem.at[0, slot]))
            out.append(pltpu.make_async_copy(ck_hbm.at[pg], kbuf.at[slot, :, pl.ds(j * page, page)], sem.at[1, slot]))
        return out

    def start(g):
        @pl.when(g < total)
        def _():
            for c in copies(g, g % nslot):
                c.start()

    @pl.when(b == 0)
    def _():
        for g0 in range(nslot - 1):
            start(jnp.int32(g0))

    q = q_ref[0]
    q_lat = q[:, :KV_LORA]
    q_rope = q[:, KV_LORA:KV_LORA + QK_ROPE]
    m_sc[...] = jnp.full(m_sc.shape, -jnp.inf, F32)
    l_sc[...] = jnp.zeros(l_sc.shape, F32)
    acc_sc[...] = jnp.zeros(acc_sc.shape, F32)

    def chunk(c, carry):
        g = b * n_chunks + c
        start(g + nslot - 1)
        slot = g % nslot
        for cp in copies(g, slot):
            cp.wait()
        kc = cbuf[slot].astype(BF16)
        kr_t = kbuf[slot].astype(BF16)
        s = _dot_nt(q_lat, kc) + _dot(q_rope, kr_t)
        m = m_sc[...]
        m_new = jnp.maximum(m, jnp.max(s, axis=-1, keepdims=True))
        alpha = jnp.exp2(m - m_new)
        p = jnp.exp2(s - m_new)
        l_sc[...] = alpha * l_sc[...] + jnp.sum(p, axis=-1, keepdims=True)
        acc_sc[...] = alpha * acc_sc[...] + _dot(p.astype(BF16), kc)
        m_sc[...] = m_new
        return carry

    lax.fori_loop(0, n_chunks, chunk, 0)

    ckvn = ckvn_ref[0]
    krn = krn_ref[0][:, ROPE_LANE0:ROPE_LANE0 + QK_ROPE]
    s_new = (jnp.sum(q_lat.astype(F32) * ckvn, axis=-1, keepdims=True)
             + jnp.sum(q_rope.astype(F32) * krn, axis=-1, keepdims=True))
    m = m_sc[...]
    m_new = jnp.maximum(m, s_new)
    alpha = jnp.exp2(m - m_new)
    p_new = jnp.exp2(s_new - m_new)
    l = alpha * l_sc[...] + p_new
    o_ref[0] = (alpha * acc_sc[...] + p_new * ckvn) / l


def _paged(q_abs, ckv_new, krp_new, cache_ckv, cache_kr, page_table, *, ch, nslot):
    nb, n_pages = page_table.shape
    page = cache_ckv.shape[1]
    assert cache_kr.shape[1:] == (QK_ROPE, page)
    n_chunks = n_pages // ch
    grid_spec = pltpu.PrefetchScalarGridSpec(
        num_scalar_prefetch=1, grid=(nb,),
        in_specs=[pl.BlockSpec((1, MLA_HEADS, Q_ABS), lambda b, pt: (b, 0, 0)),
                  pl.BlockSpec((1, 1, KV_LORA), lambda b, pt: (b, 0, 0)),
                  pl.BlockSpec((1, 1, LANE), lambda b, pt: (b, 0, 0)),
                  pl.BlockSpec(memory_space=pl.ANY), pl.BlockSpec(memory_space=pl.ANY)],
        out_specs=pl.BlockSpec((1, MLA_HEADS, KV_LORA), lambda b, pt: (b, 0, 0)),
        scratch_shapes=[pltpu.VMEM((nslot, ch * page, KV_LORA), F32), pltpu.VMEM((nslot, QK_ROPE, ch * page), F32),
                        pltpu.SemaphoreType.DMA((2, nslot)), pltpu.VMEM((MLA_HEADS, 1), F32),
                        pltpu.VMEM((MLA_HEADS, 1), F32), pltpu.VMEM((MLA_HEADS, KV_LORA), F32)])
    return pl.pallas_call(
        functools.partial(_paged_kernel, n_chunks=n_chunks, ch=ch, page=page, nslot=nslot, n_batch=nb),
        grid_spec=grid_spec, out_shape=jax.ShapeDtypeStruct((nb, MLA_HEADS, KV_LORA), F32),
        compiler_params=_cparams(("arbitrary",)), name="paged_attn",
    )(page_table, q_abs, ckv_new, krp_new, cache_ckv, cache_kr)


def _gated_norm(y, z, g):
    return _rms(y * _silu(z), g)


def _ssd_chunk_kernel(xbc_ref, z_ref, dt_ref, cw_ref, cb_ref, dtb_ref, alog_ref, dskip_ref, gn_ref,
                      y_ref, st_ref, xp_sc, *, chunk):
    c = pl.program_id(1)
    halo = 8

    @pl.when(c == 0)
    def _():
        xp_sc[0:halo, :] = jnp.zeros((halo, CONV_CH), F32)
        st_ref[...] = jnp.zeros(st_ref.shape, F32)

    xp_sc[halo:halo + chunk, :] = xbc_ref[...]
    conv = cb_ref[...]
    for kk in range(CONV_K):
        off = halo - (CONV_K - 1) + kk
        conv = conv + xp_sc[off:off + chunk, :] * cw_ref[kk:kk + 1, :]
    xp_sc[0:halo, :] = xbc_ref[chunk - halo:chunk, :]
    xbc = _silu(conv)
    xs = xbc[:, :SSM_INNER]
    dt = _softplus(dt_ref[...] + dtb_ref[...])
    da = dt * (-jnp.exp(alog_ref[...]))
    ri = lax.broadcasted_iota(jnp.int32, (chunk, chunk), 0)
    ci = lax.broadcasted_iota(jnp.int32, (chunk, chunk), 1)
    tril = ci <= ri
    cs = jnp.dot(jnp.where(tril, 1.0, 0.0), da, precision=lax.Precision.HIGHEST, preferred_element_type=F32)
    cs_t = cs.T
    dt_t = dt.T
    lane = lax.broadcasted_iota(jnp.int32, (1, LANE), 1)
    rowi = lax.broadcasted_iota(jnp.int32, (LANE, 1), 0)
    lo = lane < SSM_HEADDIM
    heads_per_group = SSM_HEADS // SSM_GROUPS
    ys = []
    for pr in range(SSM_HEADS // 2):
        g = (2 * pr) // heads_per_group
        bm = xbc[:, SSM_INNER + g * D_STATE:SSM_INNER + (g + 1) * D_STATE].astype(BF16)
        cm = xbc[:, SSM_INNER + (SSM_GROUPS + g) * D_STATE:SSM_INNER + (SSM_GROUPS + g + 1) * D_STATE].astype(BF16)
        cb = _dot_nt(cm, bm)
        xp = xs[:, pr * LANE:(pr + 1) * LANE]
        ms, xm, ecols, wcols, dlast, dsk = [], [], [], [], [], []
        for hh in range(2):
            hd = 2 * pr + hh
            col = cs[:, hd:hd + 1]
            rowv = cs_t[hd:hd + 1, :]
            seg = jnp.where(tril, col - rowv, -jnp.inf)
            ms.append((cb * jnp.exp(seg) * dt_t[hd:hd + 1, :]).astype(BF16))
            xm.append(jnp.where(lo if hh == 0 else ~lo, xp, 0.0).astype(BF16))
            last = cs[chunk - 1:chunk, hd:hd + 1]
            ecols.append(jnp.exp(col))
            wcols.append(jnp.exp(last - col) * dt[:, hd:hd + 1])
            dlast.append(jnp.exp(last))
            dsk.append(dskip_ref[:, hd:hd + 1])
        y_diag = _dot(jnp.concatenate(ms, axis=1), jnp.concatenate(xm, axis=0))
        st = st_ref[0, pr * LANE:(pr + 1) * LANE, :]
        y_off = _dot_nt(cm, st.astype(BF16)) * jnp.where(lo, ecols[0], ecols[1])
        xw = (xp * jnp.where(lo, wcols[0], wcols[1])).astype(BF16)
        st_new = _dot_tn(xw, bm)
        decay = jnp.where(rowi < SSM_HEADDIM, dlast[0], dlast[1])
        st_ref[0, pr * LANE:(pr + 1) * LANE, :] = decay * st + st_new
        ys.append(y_diag + y_off + jnp.where(lo, dsk[0], dsk[1]) * xp)
    y = jnp.concatenate(ys, axis=1)
    y_ref[...] = _gated_norm(y, z_ref[...], gn_ref[...])


def _ssd_chunk(xbc, z, dt, w, *, batch, seq):
    chunk = SSD_CHUNK if seq % SSD_CHUNK == 0 else seq
    nc = seq // chunk
    t = batch * seq
    row = lambda b, c: (b * nc + c, 0)
    return pl.pallas_call(
        functools.partial(_ssd_chunk_kernel, chunk=chunk),
        grid=(batch, nc),
        in_specs=[pl.BlockSpec((chunk, CONV_CH), row), pl.BlockSpec((chunk, SSM_INNER), row),
                  pl.BlockSpec((chunk, LANE), row), _full((CONV_K, CONV_CH)), _full((1, CONV_CH)), _full((1, LANE)),
                  _full((1, LANE)), _full((1, LANE)), _full((1, SSM_INNER))],
        out_specs=[pl.BlockSpec((chunk, SSM_INNER), row),
                   pl.BlockSpec((1, SSM_INNER, D_STATE), lambda b, c: (b, 0, 0))],
        out_shape=[jax.ShapeDtypeStruct((t, SSM_INNER), F32), jax.ShapeDtypeStruct((batch, SSM_INNER, D_STATE), F32)],
        scratch_shapes=[pltpu.VMEM((8 + chunk, CONV_CH), F32)],
        compiler_params=_cparams(("parallel", "arbitrary")), name="ssd_chunk",
    )(xbc, z, dt, w['conv_w'], w['conv_b'], w['dt_bias_p'], w['a_log_p'], w['d_skip_p'], w['norm_ssm_out'])


def _ssd_step_kernel(xbc_ref, cprev_ref, z_ref, dt_ref, st_ref, cw_ref, cb_ref, dtb_ref, alog_ref, dskip_ref, gn_ref,
                     y_ref, cnew_ref, stn_ref, *, nb):
    xraw = xbc_ref[...]
    conv = cb_ref[...] + xraw * cw_ref[CONV_K - 1:CONV_K, :]
    for kk in range(CONV_K - 1):
        conv = conv + cprev_ref[kk] * cw_ref[kk:kk + 1, :]
    for kk in range(CONV_K - 2):
        cnew_ref[kk] = cprev_ref[kk + 1]
    cnew_ref[CONV_K - 2] = xraw
    xbc = _silu(conv)
    xs = xbc[:, :SSM_INNER]
    dt = _softplus(dt_ref[...] + dtb_ref[...])
    decay = jnp.exp(dt * (-jnp.exp(alog_ref[...])))
    tok_l = lax.broadcasted_iota(jnp.int32, (1, nb), 1)
    heads_per_group = SSM_HEADS // SSM_GROUPS
    xs_t = xs.T
    ys = []
    for hd in range(SSM_HEADS):
        g = hd // heads_per_group
        bm = xbc[:, SSM_INNER + g * D_STATE:SSM_INNER + (g + 1) * D_STATE]
        cm = xbc[:, SSM_INNER + (SSM_GROUPS + g) * D_STATE:SSM_INNER + (SSM_GROUPS + g + 1) * D_STATE]
        x_t = xs_t[hd * SSM_HEADDIM:(hd + 1) * SSM_HEADDIM, :]
        dtb = bm * dt[:, hd:hd + 1]
        y_t = jnp.zeros((SSM_HEADDIM, nb), F32)
        for i in range(nb):
            new = decay[i:i + 1, hd:hd + 1] * st_ref[i, hd] + x_t[:, i:i + 1] * dtb[i:i + 1, :]
            stn_ref[i, hd] = new
            yi = lax.dot_general(new, cm, (((1,), (1,)), ((), ())), precision=lax.Precision.HIGHEST,
                                 preferred_element_type=F32)
            y_t = y_t + jnp.where(tok_l == i, yi, 0.0)
        ys.append(y_t)
    y = jnp.concatenate(ys, axis=0).T
    dsk = jnp.concatenate([jnp.broadcast_to(dskip_ref[:, hd:hd + 1], (1, SSM_HEADDIM)) for hd in range(SSM_HEADS)],
                          axis=1)
    y = y + dsk * xs
    y_ref[...] = _gated_norm(y, z_ref[...], gn_ref[...])


def _ssd_step(xbc, conv_prev, z, dt, state, w, *, nb):
    t = xbc.shape[0]
    row = lambda i: (i, 0)
    return pl.pallas_call(
        functools.partial(_ssd_step_kernel, nb=nb),
        grid=(t // nb,),
        in_specs=[pl.BlockSpec((nb, CONV_CH), row), pl.BlockSpec((CONV_K - 1, nb, CONV_CH), lambda i: (0, i, 0)),
                  pl.BlockSpec((nb, SSM_INNER), row), pl.BlockSpec((nb, LANE), row),
                  pl.BlockSpec((nb, SSM_HEADS, SSM_HEADDIM, D_STATE), lambda i: (i, 0, 0, 0)),
                  _full((CONV_K, CONV_CH)), _full((1, CONV_CH)), _full((1, LANE)), _full((1, LANE)), _full((1, LANE)),
                  _full((1, SSM_INNER))],
        out_specs=[pl.BlockSpec((nb, SSM_INNER), row), pl.BlockSpec((CONV_K - 1, nb, CONV_CH), lambda i: (0, i, 0)),
                   pl.BlockSpec((nb, SSM_HEADS, SSM_HEADDIM, D_STATE), lambda i: (i, 0, 0, 0))],
        out_shape=[jax.ShapeDtypeStruct((t, SSM_INNER), F32), jax.ShapeDtypeStruct((CONV_K - 1, t, CONV_CH), F32),
                   jax.ShapeDtypeStruct((t, SSM_HEADS, SSM_HEADDIM, D_STATE), F32)],
        compiler_params=_cparams(("parallel",)), name="ssd_step",
    )(xbc, conv_prev, z, dt, state, w['conv_w'], w['conv_b'], w['dt_bias_p'], w['a_log_p'], w['d_skip_p'],
      w['norm_ssm_out'])


def _out_proj_kernel(*refs, absorbed):
    if absorbed:
        (olat_ref, wuv_ref, yn_ref, x_ref, ga_ref, wout_ref, gffn_ref, wr_ref, br_ref, x1_ref, h_ref, route_ref) = refs
        o_attn = _mm(olat_ref[...], wuv_ref[...])
    else:
        (oat_ref, yn_ref, x_ref, ga_ref, wout_ref, gffn_ref, wr_ref, br_ref, x1_ref, h_ref, route_ref) = refs
        o_attn = oat_ref[...]
    n_attn = MLA_HEADS * V_DIM
    oa = _rms(o_attn, ga_ref[...])
    mix = _mm(oa, wout_ref[0:n_attn, :]) + _mm(yn_ref[...], wout_ref[n_attn:, :])
    x1 = x_ref[...] + mix
    x1_ref[...] = x1
    h = _rms(x1, gffn_ref[...])
    h_ref[:, 0, :] = h
    logits = jnp.dot(h, wr_ref[...], precision=lax.Precision.HIGHEST, preferred_element_type=F32) + br_ref[...]
    lane = lax.broadcasted_iota(jnp.int32, logits.shape, 1)
    neg = -jnp.inf
    gl = jnp.where(lane < N_GROUPS, logits, neg)
    gmax = jnp.max(gl, axis=-1, keepdims=True)
    g_w = 1.0 / jnp.sum(jnp.exp(gl - gmax), axis=-1, keepdims=True)
    g_idx = jnp.min(jnp.where(gl == gmax, lane, LANE), axis=-1, keepdims=True)
    e_lo = N_GROUPS + EXPERTS_PER_GROUP * g_idx
    el = jnp.where((lane >= e_lo) & (lane < e_lo + EXPERTS_PER_GROUP), logits, neg)
    e1 = jnp.max(el, axis=-1, keepdims=True)
    i1 = jnp.min(jnp.where(el == e1, lane, LANE), axis=-1, keepdims=True)
    el2 = jnp.where(lane == i1, neg, el)
    e2 = jnp.max(el2, axis=-1, keepdims=True)
    i2 = jnp.min(jnp.where(el2 == e2, lane, LANE), axis=-1, keepdims=True)
    r = jnp.exp(e2 - e1)
    c1 = g_w / (1.0 + r)
    c2 = g_w * r / (1.0 + r)
    route = jnp.where(lane == 0, (i1 - N_GROUPS).astype(F32), 0.0)
    route = jnp.where(lane == 1, (i2 - N_GROUPS).astype(F32), route)
    route = jnp.where(lane == 2, c1, route)
    route = jnp.where(lane == 3, c2, route)
    route_ref[...] = route


def _out_proj(o_in, yn, x, w, *, absorbed, tm):
    t, d = x.shape
    row = lambda i: (i, 0)
    n_mix = MLA_HEADS * V_DIM + SSM_INNER
    if absorbed:
        first = [pl.BlockSpec((tm, MLA_HEADS * KV_LORA), row), _full((MLA_HEADS * KV_LORA, MLA_HEADS * V_DIM))]
        args = [o_in, w['w_uv_bd32']]
    else:
        first = [pl.BlockSpec((tm, MLA_HEADS * V_DIM), row)]
        args = [o_in]
    w_out = w['w_out32'] if absorbed else w['w_out']
    in_specs = first + [pl.BlockSpec((tm, SSM_INNER), row), pl.BlockSpec((tm, d), row), _full((1, MLA_HEADS * V_DIM)),
                        _full((n_mix, d)), _full((1, d)), _full((d, LANE)), _full((1, LANE))]
    args += [yn, x, w['norm_attn_out'], w_out, w['norm_ffn'], w['w_router'], w['b_router']]
    return pl.pallas_call(
        functools.partial(_out_proj_kernel, absorbed=absorbed),
        grid=(t // tm,), in_specs=in_specs,
        out_specs=[pl.BlockSpec((tm, d), row), pl.BlockSpec((tm, 1, d), lambda i: (i, 0, 0)),
                   pl.BlockSpec((tm, LANE), row)],
        out_shape=[jax.ShapeDtypeStruct((t, d), F32), jax.ShapeDtypeStruct((t, 1, d), F32),
                   jax.ShapeDtypeStruct((t, LANE), F32)],
        compiler_params=_cparams(("parallel",)), name="out_proj_abs" if absorbed else "out_proj",
    )(*args)


def _row_copy(src_hbm, src_row, dst, r, sem):
    return pltpu.make_async_copy(src_hbm.at[src_row], dst.at[pl.ds(r, 1), :], sem)


def _row_gather(idx_ref, src_hbm, dst, sem, n_rows):
    for r in range(n_rows):
        _row_copy(src_hbm, idx_ref[0, 0, r], dst, r, sem).start()


def _row_gather_wait(src_hbm, dst, sem, n_rows):
    for r in range(n_rows):
        _row_copy(src_hbm, 0, dst, r, sem).wait()


def _gather_pipeline(i, n, idx_ref, idxn_ref, src_hbm, buf, sem, n_rows):
    slot = i % 2

    @pl.when(i == 0)
    def _():
        _row_gather(idx_ref, src_hbm, buf.at[0], sem.at[0], n_rows)

    _row_gather(idxn_ref, src_hbm, buf.at[1 - slot], sem.at[1 - slot], n_rows)
    _row_gather_wait(src_hbm, buf.at[slot], sem.at[slot], n_rows)
    return slot


def _gather_drain(i, n, src_hbm, buf, sem, n_rows):
    @pl.when(i == n - 1)
    def _():
        other = 1 - i % 2
        _row_gather_wait(src_hbm, buf.at[other], sem.at[other], n_rows)


def _moe_kernel(te_ref, tok_ref, tokn_ref, h_hbm, wg_ref, wu_ref, wd_ref, y_ref, xbuf, sem, *, tm):
    del te_ref
    i = pl.program_id(0)
    n = pl.num_programs(0)
    slot = _gather_pipeline(i, n, tok_ref, tokn_ref, h_hbm, xbuf, sem, tm)
    x = xbuf[slot].astype(BF16)
    act = _silu(_dot(x, wg_ref[0])) * _dot(x, wu_ref[0])
    y_ref[:, 0, :] = _dot(act.astype(BF16), wd_ref[0])
    _gather_drain(i, n, h_hbm, xbuf, sem, tm)


def _moe(h, tile_expert, row_token, w, *, tm):
    t, _, d = h.shape
    n_tiles = tile_expert.shape[0]
    grid_spec = pltpu.PrefetchScalarGridSpec(
        num_scalar_prefetch=1, grid=(n_tiles,),
        in_specs=[pl.BlockSpec((1, 1, tm), lambda i, te: (i, 0, 0), memory_space=pltpu.SMEM),
                  pl.BlockSpec((1, 1, tm), lambda i, te: (jnp.minimum(i + 1, n_tiles - 1), 0, 0),
                               memory_space=pltpu.SMEM),
                  pl.BlockSpec(memory_space=pl.ANY),
                  pl.BlockSpec((1, d, EXPERT_FF), lambda i, te: (te[i], 0, 0)),
                  pl.BlockSpec((1, d, EXPERT_FF), lambda i, te: (te[i], 0, 0)),
                  pl.BlockSpec((1, EXPERT_FF, d), lambda i, te: (te[i], 0, 0))],
        out_specs=pl.BlockSpec((tm, 1, d), lambda i, te: (i, 0, 0)),
        scratch_shapes=[pltpu.VMEM((2, tm, d), F32), pltpu.SemaphoreType.DMA((2,))])
    rt = row_token.reshape(n_tiles, 1, tm)
    return pl.pallas_call(
        functools.partial(_moe_kernel, tm=tm), grid_spec=grid_spec,
        out_shape=jax.ShapeDtypeStruct((n_tiles * tm, 1, d), F32),
        compiler_params=_cparams(("arbitrary",)), name="moe",
    )(tile_expert, rt, rt, h, w['w_exp_gate'], w['w_exp_up'], w['w_exp_down'])


def _moe_plan(route, *, tm):
    t = route.shape[0]
    eids = route[:, 0:2].astype(jnp.int32).reshape(-1)
    onehot = (eids[:, None] == jnp.arange(N_EXPERTS, dtype=jnp.int32)[None, :]).astype(F32)
    blk = _tile(2 * t, LANE)
    oh = onehot.reshape(-1, blk, N_EXPERTS)
    intra = jnp.einsum('ts,bse->bte', jnp.tril(jnp.ones((blk, blk), F32)), oh)
    tot = intra[:, -1, :]
    csum = (intra + (jnp.cumsum(tot, axis=0) - tot)[:, None, :]).reshape(-1, N_EXPERTS)
    rank = jnp.sum(onehot * (csum - 1.0), axis=1).astype(jnp.int32)
    counts = csum[-1].astype(jnp.int32)
    padded = ((counts + tm - 1) // tm) * tm
    ends = jnp.cumsum(padded)
    offs = ends - padded
    pos = offs[eids] + rank
    n_tiles = (2 * t) // tm + N_EXPERTS
    row_token = jnp.zeros((n_tiles * tm,), jnp.int32).at[pos].set(jnp.arange(2 * t, dtype=jnp.int32) // 2)
    tile_start = jnp.arange(n_tiles, dtype=jnp.int32) * tm
    tile_expert = jnp.sum((ends[None, :] <= tile_start[:, None]).astype(jnp.int32), axis=1)
    tile_expert = jnp.minimum(tile_expert, N_EXPERTS - 1)
    return tile_expert, row_token, pos.reshape(t, 2)


def _final_kernel(pos_ref, posn_ref, x1_ref, route_ref, p_ref, y_hbm, gple_ref, wgate_ref, wproj_ref, gfin_ref,
                  out_ref, ybuf, sem, *, tm):
    i = pl.program_id(0)
    n = pl.num_programs(0)
    slot = _gather_pipeline(i, n, pos_ref, posn_ref, y_hbm, ybuf, sem, 2 * tm)
    route = route_ref[...]
    c1 = route[:, 2:3]
    c2 = route[:, 3:4]
    x2 = x1_ref[...] + (c1 * ybuf[slot, 0:tm, :] + c2 * ybuf[slot, tm:2 * tm, :])
    gate_in = _rms(x2, gple_ref[...]).astype(BF16)
    gate = 1.0 / (1.0 + jnp.exp(-_dot(gate_in, wgate_ref[...])))
    x3 = x2 + gate * _dot(p_ref[...].astype(BF16), wproj_ref[...])
    out_ref[...] = _rms(x3, gfin_ref[...])
    _gather_drain(i, n, y_hbm, ybuf, sem, 2 * tm)


def _final(x1, route, p, y_sorted, pos, w, *, tm):
    t, d = x1.shape
    nt = t // tm
    ple = p.shape[1]
    pos_t = pos.reshape(nt, tm, 2).transpose(0, 2, 1).reshape(nt, 1, 2 * tm)
    row = lambda i: (i, 0)
    in_specs = [pl.BlockSpec((1, 1, 2 * tm), lambda i: (i, 0, 0), memory_space=pltpu.SMEM),
                pl.BlockSpec((1, 1, 2 * tm), lambda i: (jnp.minimum(i + 1, nt - 1), 0, 0), memory_space=pltpu.SMEM),
                pl.BlockSpec((tm, d), row), pl.BlockSpec((tm, LANE), row), pl.BlockSpec((tm, ple), row),
                pl.BlockSpec(memory_space=pl.ANY), _full((1, d)), _full((d, d)), _full((ple, d)), _full((1, d))]
    return pl.pallas_call(
        functools.partial(_final_kernel, tm=tm), grid=(nt,), in_specs=in_specs,
        out_specs=pl.BlockSpec((tm, d), row), out_shape=jax.ShapeDtypeStruct((t, d), F32),
        scratch_shapes=[pltpu.VMEM((2, 2 * tm, d), F32), pltpu.SemaphoreType.DMA((2,))],
        compiler_params=_cparams(("arbitrary",)), name="final",
    )(pos_t, pos_t, x1, route, p, y_sorted, w['norm_ple'], w['w_ple_gate'], w['w_ple_proj'], w['norm_final'])


def _pad_lanes(v, width=LANE):
    v = v.reshape(1, -1)
    return jnp.pad(v, ((0, 0), (0, width - v.shape[1])))


def _rot_cols(wr):
    half = QK_ROPE // 2
    return jnp.concatenate([-wr[..., half:], wr[..., :half]], axis=-1)


def _prep_weights(norm_mix, w_in, norm_q, w_uq, norm_kv, w_ukv, norm_attn_out, conv_w, conv_b, dt_bias, a_log, d_skip,
                  norm_ssm_out, w_out, norm_ffn, w_group_router, b_group_router, w_expert_router, b_expert_router,
                  w_exp_gate, w_exp_up, w_exp_down, norm_ple, w_ple_gate, w_ple_proj, norm_final):
    d = w_in.shape[0]
    splits = [Q_LORA, KV_LORA, QK_ROPE, SSM_INNER, CONV_CH, SSM_HEADS]
    bounds = [0]
    for s in splits:
        bounds.append(bounds[-1] + s)
    w_cq, w_ckv, w_kr, w_z, w_xbc, w_dt = [w_in[:, bounds[j]:bounds[j + 1]] for j in range(6)]
    zpad = lambda n: jnp.zeros((d, n), F32)
    place = lambda m: jnp.concatenate([zpad(ROPE_LANE0), m, zpad(LANE - ROPE_LANE0 - QK_ROPE)], axis=1)
    w_in_p = jnp.concatenate([w_cq, w_ckv, place(w_kr), place(_rot_cols(w_kr)), w_z, w_xbc, w_dt,
                              zpad(LANE - SSM_HEADS)], axis=1)
    nope, rp = w_uq[..., :QK_NOPE], w_uq[..., QK_NOPE:]
    zq = lambda n: jnp.zeros((Q_LORA, MLA_HEADS, n), F32)
    tail = HEAD_PAD - QK_NOPE - QK_ROPE
    wq_a = jnp.concatenate([nope, rp, zq(tail)], axis=-1).reshape(Q_LORA, -1)
    wq_b = jnp.concatenate([zq(QK_NOPE), _rot_cols(rp), zq(tail)], axis=-1).reshape(Q_LORA, -1)
    w_uk, w_uv = w_ukv[..., :QK_NOPE], w_ukv[..., QK_NOPE:]
    wk_p = jnp.concatenate([w_uk, jnp.zeros((KV_LORA, MLA_HEADS, HEAD_PAD - QK_NOPE), F32)], axis=-1)
    w_kv_p = jnp.concatenate([wk_p.reshape(KV_LORA, -1), w_uv.reshape(KV_LORA, -1)], axis=1).astype(BF16)
    w_abs = jnp.zeros((MLA_HEADS, HEAD_PAD, Q_ABS), F32)
    w_abs = w_abs.at[:, :QK_NOPE, :KV_LORA].set(jnp.transpose(w_uk, (1, 2, 0)))
    w_abs = w_abs.at[:, QK_NOPE:QK_NOPE + QK_ROPE, KV_LORA:KV_LORA + QK_ROPE].set(
        jnp.broadcast_to(jnp.eye(QK_ROPE, dtype=F32), (MLA_HEADS, QK_ROPE, QK_ROPE)))
    w_uv_bd = jnp.zeros((MLA_HEADS, KV_LORA, MLA_HEADS, V_DIM), F32)
    hidx = jnp.arange(MLA_HEADS)
    w_uv_bd = w_uv_bd.at[hidx, :, hidx, :].set(jnp.transpose(w_uv, (1, 0, 2)))
    w_router = jnp.concatenate([w_group_router, w_expert_router,
                                jnp.zeros((d, LANE - N_GROUPS - N_EXPERTS), F32)], axis=1)
    b_router = _pad_lanes(jnp.concatenate([b_group_router, b_expert_router]))
    return dict(
        norm_mix=norm_mix.reshape(1, -1), w_in_p=w_in_p.astype(BF16), w_in_p32=w_in_p, norm_q=norm_q.reshape(1, -1),
        wq_a=wq_a.astype(BF16), wq_b=wq_b.astype(BF16), wq_a32=wq_a, wq_b32=wq_b,
        norm_kv=norm_kv.reshape(1, -1), w_kv_p=w_kv_p, w_abs32=w_abs,
        w_uv_bd32=w_uv_bd.reshape(MLA_HEADS * KV_LORA, MLA_HEADS * V_DIM),
        norm_attn_out=norm_attn_out.reshape(1, -1), conv_w=conv_w, conv_b=conv_b.reshape(1, -1),
        dt_bias_p=_pad_lanes(dt_bias), a_log_p=_pad_lanes(a_log), d_skip_p=_pad_lanes(d_skip),
        norm_ssm_out=norm_ssm_out.reshape(1, -1), w_out=w_out.astype(BF16), w_out32=w_out,
        norm_ffn=norm_ffn.reshape(1, -1),
        w_router=w_router, b_router=b_router, w_exp_gate=w_exp_gate.astype(BF16), w_exp_up=w_exp_up.astype(BF16),
        w_exp_down=w_exp_down.astype(BF16), norm_ple=norm_ple.reshape(1, -1), w_ple_gate=w_ple_gate.astype(BF16),
        w_ple_proj=w_ple_proj.astype(BF16), norm_final=norm_final.reshape(1, -1))


def _rope_tables(pos):
    half = QK_ROPE // 2
    inv = ROPE_THETA ** (-jnp.arange(half, dtype=F32) / half)
    ang = pos.astype(F32)[:, None] * inv[None, :]
    place = lambda m: jnp.pad(jnp.concatenate([m, m], axis=1), ((0, 0), (ROPE_LANE0, LANE - ROPE_LANE0 - QK_ROPE)))
    return place(jnp.cos(ang)), place(jnp.sin(ang))


def _tile(n, pref):
    return pref if n % pref == 0 else n


def _ffn_tail(x1, h, route, p, w, *, tm_moe, tm_fin):
    tile_expert, row_token, pos = _moe_plan(route, tm=tm_moe)
    y_sorted = _moe(h, tile_expert, row_token, w, tm=tm_moe)
    return _final(x1, route, p, y_sorted, pos, w, tm=tm_fin)


def kernel(x_prompt, x_sample, p_prompt, p_sample, cache_ckv, cache_krope, state_conv, state_ssm, page_table, norm_mix, w_in, norm_q, w_uq, norm_kv, w_ukv, norm_attn_out, conv_w, conv_b, dt_bias, a_log, d_skip, norm_ssm_out, w_out, norm_ffn, w_group_router, b_group_router, w_expert_router, b_expert_router, w_exp_gate, w_exp_up, w_exp_down, norm_ple, w_ple_gate, w_ple_proj, norm_final):
    depth = w_in.shape[0]
    assert depth == 1, "single-layer stack"
    bp, sp, d = x_prompt.shape
    bs, ls, _ = x_sample.shape
    assert ls == 1, "decode step handles one new token per sequence"
    n_pool, page, _ = cache_ckv.shape[1:]
    n_pages = page_table.shape[1]
    past_len = n_pages * page
    w = _prep_weights(norm_mix[0], w_in[0], norm_q[0], w_uq[0], norm_kv[0], w_ukv[0], norm_attn_out[0], conv_w[0],
                      conv_b[0], dt_bias[0], a_log[0], d_skip[0], norm_ssm_out[0], w_out[0], norm_ffn[0],
                      w_group_router[0], b_group_router[0], w_expert_router[0], b_expert_router[0], w_exp_gate[0],
                      w_exp_up[0], w_exp_down[0], norm_ple[0], w_ple_gate[0], w_ple_proj[0], norm_final)

    tp = bp * sp
    xp = x_prompt.reshape(tp, d)
    tm_p = _tile(sp, 512)
    cos_p, sin_p = _rope_tables(jnp.arange(sp))
    q, k, v, ckv_p, krp_p, z_p, xbc_p, dt_p = _in_proj(xp, cos_p, sin_p, w, absorb=False, tm=tm_p,
                                                       n_pos_blocks=sp // tm_p)
    tq = _tile(sp, 512)
    o_attn = _flash(q, k, v, batch=bp, seq=sp, tq=tq, tk=tq)
    yn_p, ssm_p = _ssd_chunk(xbc_p, z_p, dt_p, w, batch=bp, seq=sp)
    x1_p, h_p, route_p = _out_proj(o_attn, yn_p, xp, w, absorbed=False, tm=_tile(tp, 512))
    y_prompt = _ffn_tail(x1_p, h_p, route_p, p_prompt[0].reshape(tp, -1), w, tm_moe=_tile(2 * tp, 512),
                         tm_fin=_tile(tp, 256))

    xs = x_sample.reshape(bs, d)
    cos_s, sin_s = _rope_tables(jnp.full((bs,), past_len, jnp.int32))
    q_abs, ckv_s, krp_s, z_s, xbc_s, dt_s = _in_proj(xs, cos_s, sin_s, w, absorb=True, tm=bs, n_pos_blocks=1)
    ch = 16 if n_pages % 16 == 0 else n_pages
    o_lat = _paged(q_abs.reshape(bs, MLA_HEADS, Q_ABS), ckv_s.reshape(bs, 1, KV_LORA), krp_s.reshape(bs, 1, LANE),
                   cache_ckv.reshape(n_pool, page, KV_LORA), jnp.swapaxes(cache_krope, 2, 3).reshape(n_pool, QK_ROPE, page), page_table,
                   ch=ch, nslot=3)
    conv_prev = jnp.transpose(state_conv[0], (1, 0, 2))
    yn_s, conv_new, ssm_s = _ssd_step(xbc_s, conv_prev, z_s, dt_s, state_ssm[0], w, nb=_tile(bs, 8))
    x1_s, h_s, route_s = _out_proj(o_lat.reshape(bs, MLA_HEADS * KV_LORA), yn_s, xs, w, absorbed=True, tm=bs)
    y_sample = _ffn_tail(x1_s, h_s, route_s, p_sample[0].reshape(bs, -1), w, tm_moe=_tile(2 * bs, 128),
                         tm_fin=bs)

    rope_sl = slice(ROPE_LANE0, ROPE_LANE0 + QK_ROPE)
    return (y_prompt.reshape(bp, sp, d), y_sample.reshape(bs, ls, d),
            ckv_p.reshape(1, bp, sp, KV_LORA), krp_p[:, rope_sl].reshape(1, bp, sp, QK_ROPE),
            xbc_p.reshape(bp, sp, CONV_CH)[:, sp - (CONV_K - 1):, :][None],
            ssm_p.reshape(1, bp, SSM_HEADS, SSM_HEADDIM, D_STATE),
            ckv_s.reshape(1, bs, ls, KV_LORA), krp_s[:, rope_sl].reshape(1, bs, ls, QK_ROPE),
            jnp.transpose(conv_new, (1, 0, 2))[None], ssm_s[None])
```

```python
import functools
import math

import jax
import jax.numpy as jnp
from jax import lax
from jax.experimental import pallas as pl
from jax.experimental.pallas import tpu as pltpu

F32 = jnp.float32
BF16 = jnp.bfloat16

MLA_HEADS = 8
QK_NOPE = 64
QK_ROPE = 32
V_DIM = 64
Q_LORA = 384
KV_LORA = 256
ROPE_THETA = 10000.0
ATTN_SCALE = 1.0 / math.sqrt(QK_NOPE + QK_ROPE)
LOG2E = math.log2(math.e)
Q_SCALE = ATTN_SCALE * LOG2E
SSM_HEADS = 8
SSM_HEADDIM = 64
SSM_INNER = SSM_HEADS * SSM_HEADDIM
SSM_GROUPS = 2
D_STATE = 128
CONV_K = 4
CONV_CH = SSM_INNER + 2 * SSM_GROUPS * D_STATE
SSD_CHUNK = 128
N_GROUPS = 4
EXPERTS_PER_GROUP = 4
N_EXPERTS = N_GROUPS * EXPERTS_PER_GROUP
EXPERT_FF = 512
EPS = 1e-6
PAIR_A = (0, 0, 0, 1, 1, 2)
PAIR_B = (1, 2, 3, 2, 3, 3)
PAIRS_PER_GROUP = len(PAIR_A)
N_BUCKETS = N_GROUPS * PAIRS_PER_GROUP
R_EA, R_EB, R_CA, R_CB, R_BUCKET = 0, 1, 2, 3, 4

LANE = 128
HEAD_PAD = 128
ROPE_LANE0 = QK_NOPE
Q_ABS = KV_LORA + LANE

C_CQ = 0
C_CKV = C_CQ + Q_LORA
C_KRA = C_CKV + KV_LORA
C_KRB = C_KRA + LANE
C_Z = C_KRB + LANE
C_XBC = C_Z + SSM_INNER
C_DT = C_XBC + CONV_CH
IN_COLS_P = C_DT + LANE

VMEM_LIMIT = 56 * 1024 * 1024


def _cparams(sem, vmem=VMEM_LIMIT):
    return pltpu.CompilerParams(dimension_semantics=sem, vmem_limit_bytes=vmem)


def _rms(x, g):
    return x * lax.rsqrt(jnp.mean(x * x, axis=-1, keepdims=True) + EPS) * g


def _dot(a, b):
    return jnp.dot(a, b, preferred_element_type=F32)


def _mm(a, w):
    if w.dtype == BF16:
        return jnp.dot(a.astype(BF16), w, preferred_element_type=F32)
    return jnp.dot(a, w, precision=lax.Precision.HIGHEST, preferred_element_type=F32)


def _dot_nt(a, b):
    return lax.dot_general(a, b, (((1,), (1,)), ((), ())), preferred_element_type=F32)


def _dot_tn(a, b):
    return lax.dot_general(a, b, (((0,), (0,)), ((), ())), preferred_element_type=F32)


def _silu(x):
    return x * (1.0 / (1.0 + jnp.exp(-x)))


def _softplus(x):
    return jnp.maximum(x, 0.0) + jnp.log(1.0 + jnp.exp(-jnp.abs(x)))


def _full(shape):
    nd = len(shape)
    return pl.BlockSpec(shape, lambda *a: (0,) * nd)


def _in_proj_kernel(x_ref, gmix_ref, win_ref, gq_ref, wqa_ref, wqb_ref, gkv_ref, wkv_ref, cos_ref, sin_ref,
                    *refs, absorb):
    if absorb:
        wabs_ref, q_ref, ckv_ref, krp_ref, z_ref, xbc_ref, dt_ref = refs
    else:
        q_ref, k_ref, v_ref, ckv_ref, krp_ref, z_ref, xbc_ref, dt_ref = refs
    h = _rms(x_ref[...], gmix_ref[...])
    u = _mm(h, win_ref[...])
    cq = _rms(u[:, C_CQ:C_CQ + Q_LORA], gq_ref[...])
    qa = _mm(cq, wqa_ref[...])
    qb = _mm(cq, wqb_ref[...])
    cos = cos_ref[...]
    sin = sin_ref[...]
    lane = lax.broadcasted_iota(jnp.int32, (1, LANE), 1)
    q_cos = Q_SCALE * (jnp.where(lane < QK_NOPE, 1.0, 0.0) + cos)
    q_sin = Q_SCALE * sin
    ckv = _rms(u[:, C_CKV:C_CKV + KV_LORA], gkv_ref[...])
    ckv_ref[...] = ckv
    krp = u[:, C_KRA:C_KRA + LANE] * cos + u[:, C_KRB:C_KRB + LANE] * sin
    krp_ref[...] = krp
    z_ref[...] = u[:, C_Z:C_Z + SSM_INNER]
    xbc_ref[...] = u[:, C_XBC:C_XBC + CONV_CH]
    dt_ref[...] = u[:, C_DT:C_DT + LANE]
    for hd in range(MLA_HEADS):
        sl = slice(hd * HEAD_PAD, (hd + 1) * HEAD_PAD)
        qh = qa[:, sl] * q_cos + qb[:, sl] * q_sin
        if absorb:
            q_ref[:, hd * Q_ABS:(hd + 1) * Q_ABS] = _mm(qh, wabs_ref[hd]).astype(BF16)
        else:
            q_ref[:, sl] = qh.astype(BF16)
    if not absorb:
        kv = _mm(ckv, wkv_ref[...])
        for hd in range(MLA_HEADS):
            sl = slice(hd * HEAD_PAD, (hd + 1) * HEAD_PAD)
            k_ref[:, sl] = (kv[:, sl] + krp).astype(BF16)
        v_ref[...] = kv[:, MLA_HEADS * HEAD_PAD:].astype(BF16)


def _in_proj(x, cos, sin, w, *, absorb, tm, n_pos_blocks):
    t, d = x.shape
    nt = t // tm
    row = lambda i: (i, 0)
    pos = lambda i: (i % n_pos_blocks, 0)
    in_specs = [
        pl.BlockSpec((tm, d), row), _full((1, d)), _full((d, IN_COLS_P)), _full((1, Q_LORA)),
        _full((Q_LORA, MLA_HEADS * HEAD_PAD)), _full((Q_LORA, MLA_HEADS * HEAD_PAD)), _full((1, KV_LORA)),
        _full((KV_LORA, MLA_HEADS * (HEAD_PAD + V_DIM))), pl.BlockSpec((tm, LANE), pos), pl.BlockSpec((tm, LANE), pos),
    ]
    sfx = '32' if absorb else ''
    args = [x, w['norm_mix'], w['w_in_p' + sfx], w['norm_q'], w['wq_a' + sfx], w['wq_b' + sfx], w['norm_kv'],
            w['w_kv_p'], cos, sin]
    tail_shapes = [
        jax.ShapeDtypeStruct((t, KV_LORA), F32), jax.ShapeDtypeStruct((t, LANE), F32),
        jax.ShapeDtypeStruct((t, SSM_INNER), F32), jax.ShapeDtypeStruct((t, CONV_CH), F32),
        jax.ShapeDtypeStruct((t, LANE), F32),
    ]
    tail_specs = [pl.BlockSpec((tm, KV_LORA), row), pl.BlockSpec((tm, LANE), row), pl.BlockSpec((tm, SSM_INNER), row),
                  pl.BlockSpec((tm, CONV_CH), row), pl.BlockSpec((tm, LANE), row)]
    if absorb:
        in_specs.append(_full((MLA_HEADS, HEAD_PAD, Q_ABS)))
        args.append(w['w_abs32'])
        out_shape = [jax.ShapeDtypeStruct((t, MLA_HEADS * Q_ABS), BF16)] + tail_shapes
        out_specs = [pl.BlockSpec((tm, MLA_HEADS * Q_ABS), row)] + tail_specs
    else:
        out_shape = [jax.ShapeDtypeStruct((t, MLA_HEADS * HEAD_PAD), BF16),
                     jax.ShapeDtypeStruct((t, MLA_HEADS * HEAD_PAD), BF16),
                     jax.ShapeDtypeStruct((t, MLA_HEADS * V_DIM), BF16)] + tail_shapes
        out_specs = [pl.BlockSpec((tm, MLA_HEADS * HEAD_PAD), row), pl.BlockSpec((tm, MLA_HEADS * HEAD_PAD), row),
                     pl.BlockSpec((tm, MLA_HEADS * V_DIM), row)] + tail_specs
    return pl.pallas_call(
        functools.partial(_in_proj_kernel, absorb=absorb),
        grid=(nt,), in_specs=in_specs, out_specs=out_specs, out_shape=out_shape,
        compiler_params=_cparams(("parallel",)), name="in_proj_abs" if absorb else "in_proj",
    )(*args)


def _flash_kernel(q_ref, k_ref, v_ref, o_ref, *, tq, tk):
    qi = pl.program_id(2)
    lane = lax.broadcasted_iota(jnp.int32, (1, LANE), 1)
    qs = [q_ref[:, hh * HEAD_PAD:(hh + 1) * HEAD_PAD] for hh in range(2)]

    def step(ki, carry, masked):
        ks = pl.multiple_of(ki * tk, tk)
        v = v_ref[pl.ds(ks, tk), :]
        if masked:
            row = qi * tq + lax.broadcasted_iota(jnp.int32, (tq, tk), 0)
            col = ks + lax.broadcasted_iota(jnp.int32, (tq, tk), 1)
            keep = col <= row
        out = []
        for hh in range(2):
            m, l, acc = carry[hh]
            s = _dot_nt(qs[hh], k_ref[pl.ds(ks, tk), hh * HEAD_PAD:(hh + 1) * HEAD_PAD])
            if masked:
                s = jnp.where(keep, s, -jnp.inf)
            m_new = jnp.maximum(m, jnp.max(s, axis=-1, keepdims=True))
            alpha = jnp.exp2(m - m_new)
            p = jnp.exp2(s - m_new)
            l = alpha * l + jnp.sum(p, axis=-1, keepdims=True)
            acc = alpha * acc + _dot(p.astype(BF16), v)
            out.append((m_new, l, acc))
        return tuple(out)

    init1 = (jnp.full((tq, 1), -jnp.inf, F32), jnp.zeros((tq, 1), F32), jnp.zeros((tq, 2 * V_DIM), F32))
    n_full = (qi * tq) // tk
    carry = lax.fori_loop(0, n_full, functools.partial(step, masked=False), (init1, init1))
    n_masked = (tq + tk - 1) // tk
    for d in range(n_masked):
        carry = step(n_full + d, carry, True)
    o0 = carry[0][2] / carry[0][1]
    o1 = carry[1][2] / carry[1][1]
    o_ref[...] = jnp.where(lane < V_DIM, o0, o1)


def _flash(q, k, v, *, batch, seq, tq, tk):
    t = batch * seq
    nq = seq // tq
    npair = MLA_HEADS // 2
    return pl.pallas_call(
        functools.partial(_flash_kernel, tq=tq, tk=tk),
        grid=(batch, npair, nq),
        in_specs=[pl.BlockSpec((tq, 2 * HEAD_PAD), lambda b, hp, i: (b * nq + i, hp)),
                  pl.BlockSpec((seq, 2 * HEAD_PAD), lambda b, hp, i: (b, hp)),
                  pl.BlockSpec((seq, 2 * V_DIM), lambda b, hp, i: (b, hp))],
        out_specs=pl.BlockSpec((tq, 2 * V_DIM), lambda b, hp, i: (b * nq + i, hp)),
        out_shape=jax.ShapeDtypeStruct((t, MLA_HEADS * V_DIM), F32),
        compiler_params=_cparams(("parallel", "parallel", "arbitrary")), name="flash",
    )(q, k, v)


def _paged_kernel(pt_ref, q_ref, ckvn_ref, krn_ref, cc_hbm, ck_hbm, o_ref, cbuf, kbuf, sem,
                  *, n_chunks, ch, page, nslot, n_steps, eb):
    step = pl.program_id(0)
    total = n_steps * n_chunks

    def copies(g, slot):
        gs = g // n_chunks
        gc = g % n_chunks
        out = []
        for e in range(eb):
            for j in range(ch):
                pg = pt_ref[gs * eb + e, gc * ch + j]
                out.append(pltpu.make_async_copy(cc_hbm.at[pg], cbuf.at[slot, e, pl.ds(j * page, page), :],
                                                 sem.at[0, slot]))
                out.append(pltpu.make_async_copy(ck_hbm.at[pg], kbuf.at[slot, e, :, pl.ds(j * page, page)],
                                                 sem.at[1, slot]))
        return out

    def start(g):
        @pl.when(g < total)
        def _():
            for c in copies(g, g % nslot):
                c.start()

    @pl.when(step == 0)
    def _():
        for g0 in range(nslot - 1):
            start(jnp.int32(g0))

    qs = [q_ref[e] for e in range(eb)]

    def chunk(c, carry):
        g = step * n_chunks + c
        start(g + nslot - 1)
        slot = g % nslot
        for cp in copies(g, slot):
            cp.wait()
        new = []
        for e in range(eb):
            m, l, acc = carry[e]
            kc = cbuf[slot, e].astype(BF16)
            kr_t = kbuf[slot, e].astype(BF16)
            s = _dot_nt(qs[e][:, :KV_LORA], kc) + _dot(qs[e][:, KV_LORA:KV_LORA + QK_ROPE], kr_t)
            m_new = jnp.maximum(m, jnp.max(s, axis=-1, keepdims=True))
            alpha = jnp.exp2(m - m_new)
            p = jnp.exp2(s - m_new)
            l = alpha * l + jnp.sum(p, axis=-1, keepdims=True)
            acc = alpha * acc + _dot(p.astype(BF16), kc)
            new.append((m_new, l, acc))
        return tuple(new)

    init1 = (jnp.full((MLA_HEADS, 1), -jnp.inf, F32), jnp.zeros((MLA_HEADS, 1), F32),
             jnp.zeros((MLA_HEADS, KV_LORA), F32))
    carry = lax.fori_loop(0, n_chunks, chunk, (init1,) * eb)

    for e in range(eb):
        m, l, acc = carry[e]
        q_lat = qs[e][:, :KV_LORA].astype(F32)
        q_rope = qs[e][:, KV_LORA:KV_LORA + QK_ROPE].astype(F32)
        ckvn = ckvn_ref[e]
        krn = krn_ref[e][:, ROPE_LANE0:ROPE_LANE0 + QK_ROPE]
        s_new = jnp.sum(q_lat * ckvn, axis=-1, keepdims=True) + jnp.sum(q_rope * krn, axis=-1, keepdims=True)
        m_new = jnp.maximum(m, s_new)
        alpha = jnp.exp2(m - m_new)
        p_new = jnp.exp2(s_new - m_new)
        o_ref[e] = (alpha * acc + p_new * ckvn) / (alpha * l + p_new)


def _paged(q_abs, ckv_new, krp_new, cache_ckv, cache_kr, page_table, *, ch, nslot, eb):
    nb, n_pages = page_table.shape
    page = cache_ckv.shape[1]
    assert cache_kr.shape[1:] == (QK_ROPE, page)
    n_chunks = n_pages // ch
    n_steps = nb // eb
    blk = lambda *tail: pl.BlockSpec((eb,) + tail, lambda s, pt: (s,) + (0,) * len(tail))
    grid_spec = pltpu.PrefetchScalarGridSpec(
        num_scalar_prefetch=1, grid=(n_steps,),
        in_specs=[blk(MLA_HEADS, Q_ABS), blk(1, KV_LORA), blk(1, LANE),
                  pl.BlockSpec(memory_space=pl.ANY), pl.BlockSpec(memory_space=pl.ANY)],
        out_specs=blk(MLA_HEADS, KV_LORA),
        scratch_shapes=[pltpu.VMEM((nslot, eb, ch * page, KV_LORA), F32),
                        pltpu.VMEM((nslot, eb, QK_ROPE, ch * page), F32), pltpu.SemaphoreType.DMA((2, nslot))])
    return pl.pallas_call(
        functools.partial(_paged_kernel, n_chunks=n_chunks, ch=ch, page=page, nslot=nslot, n_steps=n_steps, eb=eb),
        grid_spec=grid_spec, out_shape=jax.ShapeDtypeStruct((nb, MLA_HEADS, KV_LORA), F32),
        compiler_params=_cparams(("arbitrary",)), name="paged_attn",
    )(page_table, q_abs, ckv_new, krp_new, cache_ckv, cache_kr)


def _gated_norm(y, z, g):
    return _rms(y * _silu(z), g)


def _ssd_chunk_kernel(xbc_ref, z_ref, dt_ref, cw_ref, cb_ref, dtb_ref, alog_ref, dskip_ref, gn_ref,
                      y_ref, st_ref, xp_sc, *, chunk):
    c = pl.program_id(1)
    halo = 8

    @pl.when(c == 0)
    def _():
        xp_sc[0:halo, :] = jnp.zeros((halo, CONV_CH), F32)
        st_ref[...] = jnp.zeros(st_ref.shape, F32)

    xp_sc[halo:halo + chunk, :] = xbc_ref[...]
    conv = cb_ref[...]
    for kk in range(CONV_K):
        off = halo - (CONV_K - 1) + kk
        conv = conv + xp_sc[off:off + chunk, :] * cw_ref[kk:kk + 1, :]
    xp_sc[0:halo, :] = xbc_ref[chunk - halo:chunk, :]
    xbc = _silu(conv)
    xs = xbc[:, :SSM_INNER]
    dt = _softplus(dt_ref[...] + dtb_ref[...])
    da = dt * (-jnp.exp(alog_ref[...]))
    ri = lax.broadcasted_iota(jnp.int32, (chunk, chunk), 0)
    ci = lax.broadcasted_iota(jnp.int32, (chunk, chunk), 1)
    tril = ci <= ri
    cs = jnp.dot(jnp.where(tril, 1.0, 0.0), da, precision=lax.Precision.HIGHEST, preferred_element_type=F32)
    cs_t = cs.T
    dt_t = dt.T
    lane = lax.broadcasted_iota(jnp.int32, (1, LANE), 1)
    rowi = lax.broadcasted_iota(jnp.int32, (LANE, 1), 0)
    lo = lane < SSM_HEADDIM
    heads_per_group = SSM_HEADS // SSM_GROUPS
    ys = []
    for pr in range(SSM_HEADS // 2):
        g = (2 * pr) // heads_per_group
        bm = xbc[:, SSM_INNER + g * D_STATE:SSM_INNER + (g + 1) * D_STATE].astype(BF16)
        cm = xbc[:, SSM_INNER + (SSM_GROUPS + g) * D_STATE:SSM_INNER + (SSM_GROUPS + g + 1) * D_STATE].astype(BF16)
        cb = _dot_nt(cm, bm)
        xp = xs[:, pr * LANE:(pr + 1) * LANE]
        ms, xm, ecols, wcols, dlast, dsk = [], [], [], [], [], []
        for hh in range(2):
            hd = 2 * pr + hh
            col = cs[:, hd:hd + 1]
            rowv = cs_t[hd:hd + 1, :]
            seg = jnp.where(tril, col - rowv, -jnp.inf)
            ms.append((cb * jnp.exp(seg) * dt_t[hd:hd + 1, :]).astype(BF16))
            xm.append(jnp.where(lo if hh == 0 else ~lo, xp, 0.0).astype(BF16))
            last = cs[chunk - 1:chunk, hd:hd + 1]
            ecols.append(jnp.exp(col))
            wcols.append(jnp.exp(last - col) * dt[:, hd:hd + 1])
            dlast.append(jnp.exp(last))
            dsk.append(dskip_ref[:, hd:hd + 1])
        y_diag = _dot(jnp.concatenate(ms, axis=1), jnp.concatenate(xm, axis=0))
        st = st_ref[0, pr * LANE:(pr + 1) * LANE, :]
        y_off = _dot_nt(cm, st.astype(BF16)) * jnp.where(lo, ecols[0], ecols[1])
        xw = (xp * jnp.where(lo, wcols[0], wcols[1])).astype(BF16)
        st_new = _dot_tn(xw, bm)
        decay = jnp.where(rowi < SSM_HEADDIM, dlast[0], dlast[1])
        st_ref[0, pr * LANE:(pr + 1) * LANE, :] = decay * st + st_new
        ys.append(y_diag + y_off + jnp.where(lo, dsk[0], dsk[1]) * xp)
    y = jnp.concatenate(ys, axis=1)
    y_ref[...] = _gated_norm(y, z_ref[...], gn_ref[...])


def _ssd_chunk(xbc, z, dt, w, *, batch, seq):
    chunk = SSD_CHUNK if seq % SSD_CHUNK == 0 else seq
    nc = seq // chunk
    t = batch * seq
    row = lambda b, c: (b * nc + c, 0)
    return pl.pallas_call(
        functools.partial(_ssd_chunk_kernel, chunk=chunk),
        grid=(batch, nc),
        in_specs=[pl.BlockSpec((chunk, CONV_CH), row), pl.BlockSpec((chunk, SSM_INNER), row),
                  pl.BlockSpec((chunk, LANE), row), _full((CONV_K, CONV_CH)), _full((1, CONV_CH)), _full((1, LANE)),
                  _full((1, LANE)), _full((1, LANE)), _full((1, SSM_INNER))],
        out_specs=[pl.BlockSpec((chunk, SSM_INNER), row),
                   pl.BlockSpec((1, SSM_INNER, D_STATE), lambda b, c: (b, 0, 0))],
        out_shape=[jax.ShapeDtypeStruct((t, SSM_INNER), F32), jax.ShapeDtypeStruct((batch, SSM_INNER, D_STATE), F32)],
        scratch_shapes=[pltpu.VMEM((8 + chunk, CONV_CH), F32)],
        compiler_params=_cparams(("parallel", "arbitrary")), name="ssd_chunk",
    )(xbc, z, dt, w['conv_w'], w['conv_b'], w['dt_bias_p'], w['a_log_p'], w['d_skip_p'], w['norm_ssm_out'])


def _ssd_step_kernel(xbc_ref, cprev_ref, z_ref, dt_ref, st_ref, cw_ref, cb_ref, dtb_ref, alog_ref, dskip_ref, gn_ref,
                     y_ref, cnew_ref, stn_ref, *, nb):
    xraw = xbc_ref[...]
    conv = cb_ref[...] + xraw * cw_ref[CONV_K - 1:CONV_K, :]
    for kk in range(CONV_K - 1):
        conv = conv + cprev_ref[kk] * cw_ref[kk:kk + 1, :]
    for kk in range(CONV_K - 2):
        cnew_ref[kk] = cprev_ref[kk + 1]
    cnew_ref[CONV_K - 2] = xraw
    xbc = _silu(conv)
    xs = xbc[:, :SSM_INNER]
    dt = _softplus(dt_ref[...] + dtb_ref[...])
    decay = jnp.exp(dt * (-jnp.exp(alog_ref[...])))
    tok_l = lax.broadcasted_iota(jnp.int32, (1, nb), 1)
    heads_per_group = SSM_HEADS // SSM_GROUPS
    xs_t = xs.T
    ys = []
    for hd in range(SSM_HEADS):
        g = hd // heads_per_group
        bm = xbc[:, SSM_INNER + g * D_STATE:SSM_INNER + (g + 1) * D_STATE]
        cm = xbc[:, SSM_INNER + (SSM_GROUPS + g) * D_STATE:SSM_INNER + (SSM_GROUPS + g + 1) * D_STATE]
        x_t = xs_t[hd * SSM_HEADDIM:(hd + 1) * SSM_HEADDIM, :]
        dtb = bm * dt[:, hd:hd + 1]
        y_t = jnp.zeros((SSM_HEADDIM, nb), F32)
        for i in range(nb):
            new = decay[i:i + 1, hd:hd + 1] * st_ref[i, hd] + x_t[:, i:i + 1] * dtb[i:i + 1, :]
            stn_ref[i, hd] = new
            yi = lax.dot_general(new, cm, (((1,), (1,)), ((), ())), precision=lax.Precision.HIGHEST,
                                 preferred_element_type=F32)
            y_t = y_t + jnp.where(tok_l == i, yi, 0.0)
        ys.append(y_t)
    y = jnp.concatenate(ys, axis=0).T
    dsk = jnp.concatenate([jnp.broadcast_to(dskip_ref[:, hd:hd + 1], (1, SSM_HEADDIM)) for hd in range(SSM_HEADS)],
                          axis=1)
    y = y + dsk * xs
    y_ref[...] = _gated_norm(y, z_ref[...], gn_ref[...])


def _ssd_step(xbc, conv_prev, z, dt, state, w, *, nb):
    t = xbc.shape[0]
    row = lambda i: (i, 0)
    return pl.pallas_call(
        functools.partial(_ssd_step_kernel, nb=nb),
        grid=(t // nb,),
        in_specs=[pl.BlockSpec((nb, CONV_CH), row), pl.BlockSpec((CONV_K - 1, nb, CONV_CH), lambda i: (0, i, 0)),
                  pl.BlockSpec((nb, SSM_INNER), row), pl.BlockSpec((nb, LANE), row),
                  pl.BlockSpec((nb, SSM_HEADS, SSM_HEADDIM, D_STATE), lambda i: (i, 0, 0, 0)),
                  _full((CONV_K, CONV_CH)), _full((1, CONV_CH)), _full((1, LANE)), _full((1, LANE)), _full((1, LANE)),
                  _full((1, SSM_INNER))],
        out_specs=[pl.BlockSpec((nb, SSM_INNER), row), pl.BlockSpec((CONV_K - 1, nb, CONV_CH), lambda i: (0, i, 0)),
                   pl.BlockSpec((nb, SSM_HEADS, SSM_HEADDIM, D_STATE), lambda i: (i, 0, 0, 0))],
        out_shape=[jax.ShapeDtypeStruct((t, SSM_INNER), F32), jax.ShapeDtypeStruct((CONV_K - 1, t, CONV_CH), F32),
                   jax.ShapeDtypeStruct((t, SSM_HEADS, SSM_HEADDIM, D_STATE), F32)],
        compiler_params=_cparams(("parallel",)), name="ssd_step",
    )(xbc, conv_prev, z, dt, state, w['conv_w'], w['conv_b'], w['dt_bias_p'], w['a_log_p'], w['d_skip_p'],
      w['norm_ssm_out'])


def _out_proj_kernel(*refs, absorbed, aliased):
    if aliased:
        refs = refs[:-4] + refs[-3:]
    if absorbed:
        (olat_ref, wuv_ref, yn_ref, x_ref, ga_ref, wout_ref, gffn_ref, wr_ref, br_ref, x1_ref, h_ref, route_ref) = refs
        o_attn = _mm(olat_ref[...], wuv_ref[...])
    else:
        (oat_ref, yn_ref, x_ref, ga_ref, wout_ref, gffn_ref, wr_ref, br_ref, x1_ref, h_ref, route_ref) = refs
        o_attn = oat_ref[...]
    n_attn = MLA_HEADS * V_DIM
    d_model = x_ref.shape[1]
    oa = _rms(o_attn, ga_ref[...])
    mix = _mm(oa, wout_ref[0:n_attn, :]) + _mm(yn_ref[...], wout_ref[n_attn:, :])
    x1 = x_ref[...] + mix
    x1_ref[...] = x1
    h = _rms(x1, gffn_ref[...])
    h_ref[:, 0, 0:d_model] = h
    logits = jnp.dot(h, wr_ref[...], precision=lax.Precision.HIGHEST, preferred_element_type=F32) + br_ref[...]
    lane = lax.broadcasted_iota(jnp.int32, logits.shape, 1)
    neg = -jnp.inf
    gl = jnp.where(lane < N_GROUPS, logits, neg)
    gmax = jnp.max(gl, axis=-1, keepdims=True)
    g_w = 1.0 / jnp.sum(jnp.exp(gl - gmax), axis=-1, keepdims=True)
    g_idx = jnp.min(jnp.where(gl == gmax, lane, LANE), axis=-1, keepdims=True)
    e_lo = N_GROUPS + EXPERTS_PER_GROUP * g_idx
    el = jnp.where((lane >= e_lo) & (lane < e_lo + EXPERTS_PER_GROUP), logits, neg)
    e1 = jnp.max(el, axis=-1, keepdims=True)
    i1 = jnp.min(jnp.where(el == e1, lane, LANE), axis=-1, keepdims=True)
    el2 = jnp.where(lane == i1, neg, el)
    e2 = jnp.max(el2, axis=-1, keepdims=True)
    i2 = jnp.min(jnp.where(el2 == e2, lane, LANE), axis=-1, keepdims=True)
    r = jnp.exp(e2 - e1)
    c1 = g_w / (1.0 + r)
    c2 = g_w * r / (1.0 + r)
    first = i1 < i2
    ia = jnp.where(first, i1, i2)
    ib = jnp.where(first, i2, i1)
    ca = jnp.where(first, c1, c2)
    cb = jnp.where(first, c2, c1)
    la = ia - e_lo
    lb = ib - e_lo
    pair = jnp.where(la == 0, 0, jnp.where(la == 1, 3, 5)) + (lb - la - 1)
    bucket = g_idx * PAIRS_PER_GROUP + pair
    route = jnp.where(lane == R_EA, (ia - N_GROUPS).astype(F32), 0.0)
    route = jnp.where(lane == R_EB, (ib - N_GROUPS).astype(F32), route)
    route = jnp.where(lane == R_CA, ca, route)
    route = jnp.where(lane == R_CB, cb, route)
    route = jnp.where(lane == R_BUCKET, bucket.astype(F32), route)
    route_ref[...] = route
    h_ref[:, 0, d_model:d_model + LANE] = route


def _out_proj(o_in, yn, x, w, *, absorbed, tm, h_rows, h_buf=None):
    t, d = x.shape
    row = lambda i: (i, 0)
    h_blk0 = 0 if h_buf is None else (h_rows - t) // tm
    n_mix = MLA_HEADS * V_DIM + SSM_INNER
    if absorbed:
        first = [pl.BlockSpec((tm, MLA_HEADS * KV_LORA), row), _full((MLA_HEADS * KV_LORA, MLA_HEADS * V_DIM))]
        args = [o_in, w['w_uv_bd32']]
    else:
        first = [pl.BlockSpec((tm, MLA_HEADS * V_DIM), row)]
        args = [o_in]
    w_out = w['w_out32'] if absorbed else w['w_out']
    in_specs = first + [pl.BlockSpec((tm, SSM_INNER), row), pl.BlockSpec((tm, d), row), _full((1, MLA_HEADS * V_DIM)),
                        _full((n_mix, d)), _full((1, d)), _full((d, LANE)), _full((1, LANE))]
    args += [yn, x, w['norm_attn_out'], w_out, w['norm_ffn'], w['w_router'], w['b_router']]
    aliases = {}
    if h_buf is not None:
        assert (h_rows - t) % tm == 0
        in_specs.append(pl.BlockSpec(memory_space=pl.ANY))
        aliases = {len(args): 1}
        args.append(h_buf)
    return pl.pallas_call(
        functools.partial(_out_proj_kernel, absorbed=absorbed, aliased=h_buf is not None),
        grid=(t // tm,), in_specs=in_specs,
        out_specs=[pl.BlockSpec((tm, d), row), pl.BlockSpec((tm, 1, d + LANE), lambda i: (h_blk0 + i, 0, 0)),
                   pl.BlockSpec((tm, LANE), row)],
        out_shape=[jax.ShapeDtypeStruct((t, d), F32), jax.ShapeDtypeStruct((h_rows, 1, d + LANE), F32),
                   jax.ShapeDtypeStruct((t, LANE), F32)],
        input_output_aliases=aliases,
        compiler_params=_cparams(("parallel",)), name="out_proj_abs" if absorbed else "out_proj",
    )(*args)


def _row_copy(src_hbm, src_row, dst, r, sem):
    return pltpu.make_async_copy(src_hbm.at[src_row], dst.at[pl.ds(r, 1), :], sem)


def _row_gather(idx_ref, src_hbm, dst, sem, r0, r1):
    for r in range(r0, r1):
        _row_copy(src_hbm, idx_ref[0, 0, r], dst, r, sem).start()


def _row_gather_wait(src_hbm, dst, sem, n_rows):
    for r in range(n_rows):
        _row_copy(src_hbm, 0, dst, r, sem).wait()


def _gather_begin(i, idx_ref, src_hbm, buf, sem, n_rows):
    slot = i % 2

    @pl.when(i == 0)
    def _():
        _row_gather(idx_ref, src_hbm, buf.at[0], sem.at[0], 0, n_rows)

    _row_gather_wait(src_hbm, buf.at[slot], sem.at[slot], n_rows)
    return slot


def _gather_next(slot, idxn_ref, src_hbm, buf, sem, n_rows):
    _row_gather(idxn_ref, src_hbm, buf.at[1 - slot], sem.at[1 - slot], 0, n_rows)


def _gather_drain(i, n, src_hbm, buf, sem, n_rows):
    @pl.when(i == n - 1)
    def _():
        other = 1 - i % 2
        _row_gather_wait(src_hbm, buf.at[other], sem.at[other], n_rows)


def _moe_kernel(ta_ref, tb_ref, tok_ref, tokn_ref, h_hbm, wga_ref, wua_ref, wda_ref, wgb_ref, wub_ref, wdb_ref,
                y_ref, xbuf, sem, *, tm):
    del ta_ref, tb_ref
    i = pl.program_id(0)
    n = pl.num_programs(0)
    slot = _gather_begin(i, tok_ref, h_hbm, xbuf, sem, tm)
    d = y_ref.shape[2]
    x = xbuf[slot, :, 0:d].astype(BF16)
    rec = xbuf[slot, :, d:d + LANE]
    _gather_next(slot, tokn_ref, h_hbm, xbuf, sem, tm)
    y = None
    for (wg_ref, wu_ref, wd_ref), lane_c in (((wga_ref, wua_ref, wda_ref), R_CA), ((wgb_ref, wub_ref, wdb_ref), R_CB)):
        act = _silu(_dot(x, wg_ref[0])) * _dot(x, wu_ref[0])
        term = rec[:, lane_c:lane_c + 1] * _dot(act.astype(BF16), wd_ref[0])
        y = term if y is None else y + term
    y_ref[:, 0, :] = y
    _gather_drain(i, n, h_hbm, xbuf, sem, tm)


def _moe(h, tile_ea, tile_eb, row_token, w, *, tm):
    t, _, dp = h.shape
    d = dp - LANE
    n_tiles = tile_ea.shape[0]
    ea = lambda i, ta, tb: (ta[i], 0, 0)
    eb = lambda i, ta, tb: (tb[i], 0, 0)
    grid_spec = pltpu.PrefetchScalarGridSpec(
        num_scalar_prefetch=2, grid=(n_tiles,),
        in_specs=[pl.BlockSpec((1, 1, tm), lambda i, ta, tb: (i, 0, 0), memory_space=pltpu.SMEM),
                  pl.BlockSpec((1, 1, tm), lambda i, ta, tb: (jnp.minimum(i + 1, n_tiles - 1), 0, 0),
                               memory_space=pltpu.SMEM),
                  pl.BlockSpec(memory_space=pl.ANY),
                  pl.BlockSpec((1, d, EXPERT_FF), ea), pl.BlockSpec((1, d, EXPERT_FF), ea),
                  pl.BlockSpec((1, EXPERT_FF, d), ea),
                  pl.BlockSpec((1, d, EXPERT_FF), eb), pl.BlockSpec((1, d, EXPERT_FF), eb),
                  pl.BlockSpec((1, EXPERT_FF, d), eb)],
        out_specs=pl.BlockSpec((tm, 1, d), lambda i, ta, tb: (i, 0, 0)),
        scratch_shapes=[pltpu.VMEM((2, tm, dp), F32), pltpu.SemaphoreType.DMA((2,))])
    rt = row_token.reshape(n_tiles, 1, tm)
    wts = (w['w_exp_gate'], w['w_exp_up'], w['w_exp_down'])
    return pl.pallas_call(
        functools.partial(_moe_kernel, tm=tm), grid_spec=grid_spec,
        out_shape=jax.ShapeDtypeStruct((n_tiles * tm, 1, d), F32),
        compiler_params=_cparams(("arbitrary",)), name="moe",
    )(tile_ea, tile_eb, rt, rt, h, *wts, *wts)


def _moe_plan(route, *, tm):
    t = route.shape[0]
    bids = route[:, R_BUCKET].astype(jnp.int32)
    onehot = (bids[:, None] == jnp.arange(N_BUCKETS, dtype=jnp.int32)[None, :]).astype(F32)
    blk = _tile(t, LANE)
    oh = onehot.reshape(-1, blk, N_BUCKETS)
    intra = jnp.einsum('ts,bse->bte', jnp.tril(jnp.ones((blk, blk), F32)), oh)
    tot = intra[:, -1, :]
    csum = (intra + (jnp.cumsum(tot, axis=0) - tot)[:, None, :]).reshape(-1, N_BUCKETS)
    rank = jnp.sum(onehot * (csum - 1.0), axis=1).astype(jnp.int32)
    counts = csum[-1].astype(jnp.int32)
    padded = ((counts + tm - 1) // tm) * tm
    ends = jnp.cumsum(padded)
    offs = ends - padded
    pos = offs[bids] + rank
    n_tiles = -(-t // tm) + N_BUCKETS
    row_token = jnp.zeros((n_tiles * tm,), jnp.int32).at[pos].set(jnp.arange(t, dtype=jnp.int32))
    tile_start = jnp.arange(n_tiles, dtype=jnp.int32) * tm
    tile_bucket = jnp.sum((ends[None, :] <= tile_start[:, None]).astype(jnp.int32), axis=1)
    tile_bucket = jnp.minimum(tile_bucket, N_BUCKETS - 1)
    group0 = (tile_bucket // PAIRS_PER_GROUP) * EXPERTS_PER_GROUP
    pair = tile_bucket % PAIRS_PER_GROUP
    tile_ea = group0 + jnp.asarray(PAIR_A, jnp.int32)[pair]
    tile_eb = group0 + jnp.asarray(PAIR_B, jnp.int32)[pair]
    return tile_ea, tile_eb, row_token, pos


def _final_kernel(pos_ref, posn_ref, x1_ref, p_ref, y_hbm, gple_ref, wgate_ref, wproj_ref, gfin_ref,
                  out_ref, ybuf, sem, *, tm):
    i = pl.program_id(0)
    n = pl.num_programs(0)
    slot = _gather_begin(i, pos_ref, y_hbm, ybuf, sem, tm)
    _gather_next(slot, posn_ref, y_hbm, ybuf, sem, tm)
    x2 = x1_ref[...] + ybuf[slot]
    gate_in = _rms(x2, gple_ref[...]).astype(BF16)
    gate = 1.0 / (1.0 + jnp.exp(-_dot(gate_in, wgate_ref[...])))
    x3 = x2 + gate * _dot(p_ref[...].astype(BF16), wproj_ref[...])
    out_ref[...] = _rms(x3, gfin_ref[...])
    _gather_drain(i, n, y_hbm, ybuf, sem, tm)


def _final(x1, p, y_sorted, pos, w, *, tm):
    t, d = x1.shape
    nt = t // tm
    ple = p.shape[1]
    pos_t = pos.reshape(nt, 1, tm)
    row = lambda i: (i, 0)
    in_specs = [pl.BlockSpec((1, 1, tm), lambda i: (i, 0, 0), memory_space=pltpu.SMEM),
                pl.BlockSpec((1, 1, tm), lambda i: (jnp.minimum(i + 1, nt - 1), 0, 0), memory_space=pltpu.SMEM),
                pl.BlockSpec((tm, d), row), pl.BlockSpec((tm, ple), row),
                pl.BlockSpec(memory_space=pl.ANY), _full((1, d)), _full((d, d)), _full((ple, d)), _full((1, d))]
    return pl.pallas_call(
        functools.partial(_final_kernel, tm=tm), grid=(nt,), in_specs=in_specs,
        out_specs=pl.BlockSpec((tm, d), row), out_shape=jax.ShapeDtypeStruct((t, d), F32),
        scratch_shapes=[pltpu.VMEM((2, tm, d), F32), pltpu.SemaphoreType.DMA((2,))],
        compiler_params=_cparams(("arbitrary",)), name="final",
    )(pos_t, pos_t, x1, p, y_sorted, w['norm_ple'], w['w_ple_gate'], w['w_ple_proj'], w['norm_final'])


def _pad_lanes(v, width=LANE):
    v = v.reshape(1, -1)
    return jnp.pad(v, ((0, 0), (0, width - v.shape[1])))


def _rot_cols(wr):
    half = QK_ROPE // 2
    return jnp.concatenate([-wr[..., half:], wr[..., :half]], axis=-1)


def _prep_weights(norm_mix, w_in, norm_q, w_uq, norm_kv, w_ukv, norm_attn_out, conv_w, conv_b, dt_bias, a_log, d_skip,
                  norm_ssm_out, w_out, norm_ffn, w_group_router, b_group_router, w_expert_router, b_expert_router,
                  w_exp_gate, w_exp_up, w_exp_down, norm_ple, w_ple_gate, w_ple_proj, norm_final):
    d = w_in.shape[0]
    splits = [Q_LORA, KV_LORA, QK_ROPE, SSM_INNER, CONV_CH, SSM_HEADS]
    bounds = [0]
    for s in splits:
        bounds.append(bounds[-1] + s)
    w_cq, w_ckv, w_kr, w_z, w_xbc, w_dt = [w_in[:, bounds[j]:bounds[j + 1]] for j in range(6)]
    zpad = lambda n: jnp.zeros((d, n), F32)
    place = lambda m: jnp.concatenate([zpad(ROPE_LANE0), m, zpad(LANE - ROPE_LANE0 - QK_ROPE)], axis=1)
    w_in_p = jnp.concatenate([w_cq, w_ckv, place(w_kr), place(_rot_cols(w_kr)), w_z, w_xbc, w_dt,
                              zpad(LANE - SSM_HEADS)], axis=1)
    nope, rp = w_uq[..., :QK_NOPE], w_uq[..., QK_NOPE:]
    zq = lambda n: jnp.zeros((Q_LORA, MLA_HEADS, n), F32)
    tail = HEAD_PAD - QK_NOPE - QK_ROPE
    wq_a = jnp.concatenate([nope, rp, zq(tail)], axis=-1).reshape(Q_LORA, -1)
    wq_b = jnp.concatenate([zq(QK_NOPE), _rot_cols(rp), zq(tail)], axis=-1).reshape(Q_LORA, -1)
    w_uk, w_uv = w_ukv[..., :QK_NOPE], w_ukv[..., QK_NOPE:]
    wk_p = jnp.concatenate([w_uk, jnp.zeros((KV_LORA, MLA_HEADS, HEAD_PAD - QK_NOPE), F32)], axis=-1)
    w_kv_p = jnp.concatenate([wk_p.reshape(KV_LORA, -1), w_uv.reshape(KV_LORA, -1)], axis=1).astype(BF16)
    w_abs = jnp.zeros((MLA_HEADS, HEAD_PAD, Q_ABS), F32)
    w_abs = w_abs.at[:, :QK_NOPE, :KV_LORA].set(jnp.transpose(w_uk, (1, 2, 0)))
    w_abs = w_abs.at[:, QK_NOPE:QK_NOPE + QK_ROPE, KV_LORA:KV_LORA + QK_ROPE].set(
        jnp.broadcast_to(jnp.eye(QK_ROPE, dtype=F32), (MLA_HEADS, QK_ROPE, QK_ROPE)))
    w_uv_bd = jnp.zeros((MLA_HEADS, KV_LORA, MLA_HEADS, V_DIM), F32)
    hidx = jnp.arange(MLA_HEADS)
    w_uv_bd = w_uv_bd.at[hidx, :, hidx, :].set(jnp.transpose(w_uv, (1, 0, 2)))
    w_router = jnp.concatenate([w_group_router, w_expert_router,
                                jnp.zeros((d, LANE - N_GROUPS - N_EXPERTS), F32)], axis=1)
    b_router = _pad_lanes(jnp.concatenate([b_group_router, b_expert_router]))
    return dict(
        norm_mix=norm_mix.reshape(1, -1), w_in_p=w_in_p.astype(BF16), w_in_p32=w_in_p, norm_q=norm_q.reshape(1, -1),
        wq_a=wq_a.astype(BF16), wq_b=wq_b.astype(BF16), wq_a32=wq_a, wq_b32=wq_b,
        norm_kv=norm_kv.reshape(1, -1), w_kv_p=w_kv_p, w_abs32=w_abs,
        w_uv_bd32=w_uv_bd.reshape(MLA_HEADS * KV_LORA, MLA_HEADS * V_DIM),
        norm_attn_out=norm_attn_out.reshape(1, -1), conv_w=conv_w, conv_b=conv_b.reshape(1, -1),
        dt_bias_p=_pad_lanes(dt_bias), a_log_p=_pad_lanes(a_log), d_skip_p=_pad_lanes(d_skip),
        norm_ssm_out=norm_ssm_out.reshape(1, -1), w_out=w_out.astype(BF16), w_out32=w_out,
        norm_ffn=norm_ffn.reshape(1, -1),
        w_router=w_router, b_router=b_router, w_exp_gate=w_exp_gate.astype(BF16), w_exp_up=w_exp_up.astype(BF16),
        w_exp_down=w_exp_down.astype(BF16), norm_ple=norm_ple.reshape(1, -1), w_ple_gate=w_ple_gate.astype(BF16),
        w_ple_proj=w_ple_proj.astype(BF16), norm_final=norm_final.reshape(1, -1))


def _rope_tables(pos):
    half = QK_ROPE // 2
    inv = ROPE_THETA ** (-jnp.arange(half, dtype=F32) / half)
    ang = pos.astype(F32)[:, None] * inv[None, :]
    place = lambda m: jnp.pad(jnp.concatenate([m, m], axis=1), ((0, 0), (ROPE_LANE0, LANE - ROPE_LANE0 - QK_ROPE)))
    return place(jnp.cos(ang)), place(jnp.sin(ang))


def _tile(n, pref):
    return pref if n % pref == 0 else n


def kernel(x_prompt, x_sample, p_prompt, p_sample, cache_ckv, cache_krope, state_conv, state_ssm, page_table, norm_mix, w_in, norm_q, w_uq, norm_kv, w_ukv, norm_attn_out, conv_w, conv_b, dt_bias, a_log, d_skip, norm_ssm_out, w_out, norm_ffn, w_group_router, b_group_router, w_expert_router, b_expert_router, w_exp_gate, w_exp_up, w_exp_down, norm_ple, w_ple_gate, w_ple_proj, norm_final):
    depth = w_in.shape[0]
    assert depth == 1, "single-layer stack"
    bp, sp, d = x_prompt.shape
    bs, ls, _ = x_sample.shape
    assert ls == 1, "decode step handles one new token per sequence"
    n_pool, page, _ = cache_ckv.shape[1:]
    n_pages = page_table.shape[1]
    past_len = n_pages * page
    w = _prep_weights(norm_mix[0], w_in[0], norm_q[0], w_uq[0], norm_kv[0], w_ukv[0], norm_attn_out[0], conv_w[0],
                      conv_b[0], dt_bias[0], a_log[0], d_skip[0], norm_ssm_out[0], w_out[0], norm_ffn[0],
                      w_group_router[0], b_group_router[0], w_expert_router[0], b_expert_router[0], w_exp_gate[0],
                      w_exp_up[0], w_exp_down[0], norm_ple[0], w_ple_gate[0], w_ple_proj[0], norm_final)

    tp = bp * sp
    xp = x_prompt.reshape(tp, d)
    tm_p = _tile(sp, 512)
    cos_p, sin_p = _rope_tables(jnp.arange(sp))
    q, k, v, ckv_p, krp_p, z_p, xbc_p, dt_p = _in_proj(xp, cos_p, sin_p, w, absorb=False, tm=tm_p,
                                                       n_pos_blocks=sp // tm_p)
    tq = _tile(sp, 512)
    o_attn = _flash(q, k, v, batch=bp, seq=sp, tq=tq, tk=tq)
    yn_p, ssm_p = _ssd_chunk(xbc_p, z_p, dt_p, w, batch=bp, seq=sp)
    x1_p, h_all, route_p = _out_proj(o_attn, yn_p, xp, w, absorbed=False, tm=_tile(tp, 512), h_rows=tp + bs)

    xs = x_sample.reshape(bs, d)
    cos_s, sin_s = _rope_tables(jnp.full((bs,), past_len, jnp.int32))
    q_abs, ckv_s, krp_s, z_s, xbc_s, dt_s = _in_proj(xs, cos_s, sin_s, w, absorb=True, tm=bs, n_pos_blocks=1)
    ch = 16 if n_pages % 16 == 0 else n_pages
    o_lat = _paged(q_abs.reshape(bs, MLA_HEADS, Q_ABS), ckv_s.reshape(bs, 1, KV_LORA), krp_s.reshape(bs, 1, LANE),
                   cache_ckv.reshape(n_pool, page, KV_LORA), jnp.swapaxes(cache_krope, 2, 3).reshape(n_pool, QK_ROPE, page), page_table,
                   ch=ch, nslot=3, eb=2 if bs % 2 == 0 else 1)
    conv_prev = jnp.transpose(state_conv[0], (1, 0, 2))
    yn_s, conv_new, ssm_s = _ssd_step(xbc_s, conv_prev, z_s, dt_s, state_ssm[0], w, nb=_tile(bs, 8))
    x1_s, h_all, route_s = _out_proj(o_lat.reshape(bs, MLA_HEADS * KV_LORA), yn_s, xs, w, absorbed=True, tm=bs,
                                     h_rows=tp + bs, h_buf=h_all)

    tm_moe = 256
    tile_ea, tile_eb, row_token, pos = _moe_plan(jnp.concatenate([route_p, route_s], axis=0), tm=tm_moe)
    y_sorted = _moe(h_all, tile_ea, tile_eb, row_token, w, tm=tm_moe)
    y_prompt = _final(x1_p, p_prompt[0].reshape(tp, -1), y_sorted, pos[:tp], w, tm=_tile(tp, 512))
    y_sample = _final(x1_s, p_sample[0].reshape(bs, -1), y_sorted, pos[tp:], w, tm=bs)

    rope_sl = slice(ROPE_LANE0, ROPE_LANE0 + QK_ROPE)
    return (y_prompt.reshape(bp, sp, d), y_sample.reshape(bs, ls, d),
            ckv_p.reshape(1, bp, sp, KV_LORA), krp_p[:, rope_sl].reshape(1, bp, sp, QK_ROPE),
            xbc_p.reshape(bp, sp, CONV_CH)[:, sp - (CONV_K - 1):, :][None],
            ssm_p.reshape(1, bp, SSM_HEADS, SSM_HEADDIM, D_STATE),
            ckv_s.reshape(1, bs, ls, KV_LORA), krp_s[:, rope_sl].reshape(1, bs, ls, QK_ROPE),
            jnp.transpose(conv_new, (1, 0, 2))[None], ssm_s[None])
```

```python
import functools
import math

import jax
import jax.numpy as jnp
from jax import lax
from jax.experimental import pallas as pl
from jax.experimental.pallas import tpu as pltpu

F32 = jnp.float32
BF16 = jnp.bfloat16

MLA_HEADS = 8
QK_NOPE = 64
QK_ROPE = 32
V_DIM = 64
Q_LORA = 384
KV_LORA = 256
ROPE_THETA = 10000.0
ATTN_SCALE = 1.0 / math.sqrt(QK_NOPE + QK_ROPE)
LOG2E = math.log2(math.e)
Q_SCALE = ATTN_SCALE * LOG2E
SSM_HEADS = 8
SSM_HEADDIM = 64
SSM_INNER = SSM_HEADS * SSM_HEADDIM
SSM_GROUPS = 2
D_STATE = 128
CONV_K = 4
CONV_CH = SSM_INNER + 2 * SSM_GROUPS * D_STATE
SSD_CHUNK = 128
N_GROUPS = 4
EXPERTS_PER_GROUP = 4
N_EXPERTS = N_GROUPS * EXPERTS_PER_GROUP
EXPERT_FF = 512
EPS = 1e-6
PAIR_A = (0, 0, 0, 1, 1, 2)
PAIR_B = (1, 2, 3, 2, 3, 3)
PAIRS_PER_GROUP = len(PAIR_A)
N_BUCKETS = N_GROUPS * PAIRS_PER_GROUP
R_EA, R_EB, R_CA, R_CB, R_BUCKET = 0, 1, 2, 3, 4

LANE = 128
HEAD_PAD = 128
ROPE_LANE0 = QK_NOPE
Q_ABS = KV_LORA + LANE

C_CQ = 0
C_CKV = C_CQ + Q_LORA
C_KRA = C_CKV + KV_LORA
C_KRB = C_KRA + LANE
C_Z = C_KRB + LANE
C_XBC = C_Z + SSM_INNER
C_DT = C_XBC + CONV_CH
IN_COLS_P = C_DT + LANE

VMEM_LIMIT = 56 * 1024 * 1024


def _cparams(sem, vmem=VMEM_LIMIT):
    return pltpu.CompilerParams(dimension_semantics=sem, vmem_limit_bytes=vmem)


def _rms(x, g):
    return x * lax.rsqrt(jnp.mean(x * x, axis=-1, keepdims=True) + EPS) * g


def _dot(a, b):
    return jnp.dot(a, b, preferred_element_type=F32)


def _mm(a, w):
    if w.dtype == BF16:
        return jnp.dot(a.astype(BF16), w, preferred_element_type=F32)
    return jnp.dot(a, w, precision=lax.Precision.HIGHEST, preferred_element_type=F32)


def _dot_nt(a, b):
    return lax.dot_general(a, b, (((1,), (1,)), ((), ())), preferred_element_type=F32)


def _dot_tn(a, b):
    return lax.dot_general(a, b, (((0,), (0,)), ((), ())), preferred_element_type=F32)


def _silu(x):
    return x * (1.0 / (1.0 + jnp.exp(-x)))


def _softplus(x):
    return jnp.maximum(x, 0.0) + jnp.log(1.0 + jnp.exp(-jnp.abs(x)))


def _full(shape):
    nd = len(shape)
    return pl.BlockSpec(shape, lambda *a: (0,) * nd)


def _in_proj_kernel(x_ref, gmix_ref, win_ref, gq_ref, wqa_ref, wqb_ref, gkv_ref, wkv_ref, cos_ref, sin_ref,
                    *refs, absorb):
    if absorb:
        wabs_ref, q_ref, ckv_ref, krp_ref, z_ref, xbc_ref, dt_ref = refs
    else:
        q_ref, k_ref, v_ref, ckv_ref, krp_ref, z_ref, xbc_ref, dt_ref = refs
    h = _rms(x_ref[...], gmix_ref[...])
    u = _mm(h, win_ref[...])
    cq = _rms(u[:, C_CQ:C_CQ + Q_LORA], gq_ref[...])
    qa = _mm(cq, wqa_ref[...])
    qb = _mm(cq, wqb_ref[...])
    cos = cos_ref[...]
    sin = sin_ref[...]
    lane = lax.broadcasted_iota(jnp.int32, (1, LANE), 1)
    q_cos = Q_SCALE * (jnp.where(lane < QK_NOPE, 1.0, 0.0) + cos)
    q_sin = Q_SCALE * sin
    ckv = _rms(u[:, C_CKV:C_CKV + KV_LORA], gkv_ref[...])
    ckv_ref[...] = ckv
    krp = u[:, C_KRA:C_KRA + LANE] * cos + u[:, C_KRB:C_KRB + LANE] * sin
    krp_ref[...] = krp
    z_ref[...] = u[:, C_Z:C_Z + SSM_INNER]
    xbc_ref[...] = u[:, C_XBC:C_XBC + CONV_CH]
    dt_ref[...] = u[:, C_DT:C_DT + LANE]
    for hd in range(MLA_HEADS):
        sl = slice(hd * HEAD_PAD, (hd + 1) * HEAD_PAD)
        qh = qa[:, sl] * q_cos + qb[:, sl] * q_sin
        if absorb:
            q_ref[:, hd * Q_ABS:(hd + 1) * Q_ABS] = _mm(qh, wabs_ref[hd]).astype(BF16)
        else:
            q_ref[:, sl] = qh.astype(BF16)
    if not absorb:
        kv = _mm(ckv, wkv_ref[...])
        for hd in range(MLA_HEADS):
            sl = slice(hd * HEAD_PAD, (hd + 1) * HEAD_PAD)
            k_ref[:, sl] = (kv[:, sl] + krp).astype(BF16)
        v_ref[...] = kv[:, MLA_HEADS * HEAD_PAD:].astype(BF16)


def _in_proj(x, cos, sin, w, *, absorb, tm, n_pos_blocks):
    t, d = x.shape
    nt = t // tm
    row = lambda i: (i, 0)
    pos = lambda i: (i % n_pos_blocks, 0)
    in_specs = [
        pl.BlockSpec((tm, d), row), _full((1, d)), _full((d, IN_COLS_P)), _full((1, Q_LORA)),
        _full((Q_LORA, MLA_HEADS * HEAD_PAD)), _full((Q_LORA, MLA_HEADS * HEAD_PAD)), _full((1, KV_LORA)),
        _full((KV_LORA, MLA_HEADS * (HEAD_PAD + V_DIM))), pl.BlockSpec((tm, LANE), pos), pl.BlockSpec((tm, LANE), pos),
    ]
    sfx = '32' if absorb else ''
    args = [x, w['norm_mix'], w['w_in_p' + sfx], w['norm_q'], w['wq_a' + sfx], w['wq_b' + sfx], w['norm_kv'],
            w['w_kv_p'], cos, sin]
    tail_shapes = [
        jax.ShapeDtypeStruct((t, KV_LORA), F32), jax.ShapeDtypeStruct((t, LANE), F32),
        jax.ShapeDtypeStruct((t, SSM_INNER), F32), jax.ShapeDtypeStruct((t, CONV_CH), F32),
        jax.ShapeDtypeStruct((t, LANE), F32),
    ]
    tail_specs = [pl.BlockSpec((tm, KV_LORA), row), pl.BlockSpec((tm, LANE), row), pl.BlockSpec((tm, SSM_INNER), row),
                  pl.BlockSpec((tm, CONV_CH), row), pl.BlockSpec((tm, LANE), row)]
    if absorb:
        in_specs.append(_full((MLA_HEADS, HEAD_PAD, Q_ABS)))
        args.append(w['w_abs32'])
        out_shape = [jax.ShapeDtypeStruct((t, MLA_HEADS * Q_ABS), BF16)] + tail_shapes
        out_specs = [pl.BlockSpec((tm, MLA_HEADS * Q_ABS), row)] + tail_specs
    else:
        out_shape = [jax.ShapeDtypeStruct((t, MLA_HEADS * HEAD_PAD), BF16),
                     jax.ShapeDtypeStruct((t, MLA_HEADS * HEAD_PAD), BF16),
                     jax.ShapeDtypeStruct((t, MLA_HEADS * V_DIM), BF16)] + tail_shapes
        out_specs = [pl.BlockSpec((tm, MLA_HEADS * HEAD_PAD), row), pl.BlockSpec((tm, MLA_HEADS * HEAD_PAD), row),
                     pl.BlockSpec((tm, MLA_HEADS * V_DIM), row)] + tail_specs
    return pl.pallas_call(
        functools.partial(_in_proj_kernel, absorb=absorb),
        grid=(nt,), in_specs=in_specs, out_specs=out_specs, out_shape=out_shape,
        compiler_params=_cparams(("parallel",)), name="in_proj_abs" if absorb else "in_proj",
    )(*args)


def _flash_kernel(q_ref, k_ref, v_ref, o_ref, *, tq, tk):
    qi = pl.program_id(2)
    lane = lax.broadcasted_iota(jnp.int32, (1, LANE), 1)
    qs = [q_ref[:, hh * HEAD_PAD:(hh + 1) * HEAD_PAD] for hh in range(2)]

    def step(ki, carry, masked):
        ks = pl.multiple_of(ki * tk, tk)
        v = v_ref[pl.ds(ks, tk), :]
        if masked:
            row = qi * tq + lax.broadcasted_iota(jnp.int32, (tq, tk), 0)
            col = ks + lax.broadcasted_iota(jnp.int32, (tq, tk), 1)
            keep = col <= row
        out = []
        for hh in range(2):
            m, l, acc = carry[hh]
            s = _dot_nt(qs[hh], k_ref[pl.ds(ks, tk), hh * HEAD_PAD:(hh + 1) * HEAD_PAD])
            if masked:
                s = jnp.where(keep, s, -jnp.inf)
            m_new = jnp.maximum(m, jnp.max(s, axis=-1, keepdims=True))
            alpha = jnp.exp2(m - m_new)
            p = jnp.exp2(s - m_new)
            l = alpha * l + jnp.sum(p, axis=-1, keepdims=True)
            acc = alpha * acc + _dot(p.astype(BF16), v)
            out.append((m_new, l, acc))
        return tuple(out)

    init1 = (jnp.full((tq, 1), -jnp.inf, F32), jnp.zeros((tq, 1), F32), jnp.zeros((tq, 2 * V_DIM), F32))
    n_full = (qi * tq) // tk
    carry = lax.fori_loop(0, n_full, functools.partial(step, masked=False), (init1, init1))
    n_masked = (tq + tk - 1) // tk
    for d in range(n_masked):
        carry = step(n_full + d, carry, True)
    o0 = carry[0][2] / carry[0][1]
    o1 = carry[1][2] / carry[1][1]
    o_ref[...] = jnp.where(lane < V_DIM, o0, o1)


def _flash(q, k, v, *, batch, seq, tq, tk):
    t = batch * seq
    nq = seq // tq
    npair = MLA_HEADS // 2
    return pl.pallas_call(
        functools.partial(_flash_kernel, tq=tq, tk=tk),
        grid=(batch, npair, nq),
        in_specs=[pl.BlockSpec((tq, 2 * HEAD_PAD), lambda b, hp, i: (b * nq + i, hp)),
                  pl.BlockSpec((seq, 2 * HEAD_PAD), lambda b, hp, i: (b, hp)),
                  pl.BlockSpec((seq, 2 * V_DIM), lambda b, hp, i: (b, hp))],
        out_specs=pl.BlockSpec((tq, 2 * V_DIM), lambda b, hp, i: (b * nq + i, hp)),
        out_shape=jax.ShapeDtypeStruct((t, MLA_HEADS * V_DIM), F32),
        compiler_params=_cparams(("parallel", "parallel", "arbitrary")), name="flash",
    )(q, k, v)


def _paged_kernel(pt_ref, q_ref, ckvn_ref, krn_ref, cc_hbm, ck_hbm, o_ref, cbuf, kbuf, sem,
                  *, n_chunks, ch, page, nslot, n_steps, eb):
    step = pl.program_id(0)
    total = n_steps * n_chunks

    def copies(g, slot):
        gs = g // n_chunks
        gc = g % n_chunks
        out = []
        for e in range(eb):
            for j in range(ch):
                pg = pt_ref[gs * eb + e, gc * ch + j]
                out.append(pltpu.make_async_copy(cc_hbm.at[pg], cbuf.at[slot, e, pl.ds(j * page, page), :],
                                                 sem.at[0, slot]))
                out.append(pltpu.make_async_copy(ck_hbm.at[pg], kbuf.at[slot, e, :, pl.ds(j * page, page)],
                                                 sem.at[1, slot]))
        return out

    def start(g):
        @pl.when(g < total)
        def _():
            for c in copies(g, g % nslot):
                c.start()

    @pl.when(step == 0)
    def _():
        for g0 in range(nslot - 1):
            start(jnp.int32(g0))

    qs = [q_ref[e] for e in range(eb)]

    def chunk(c, carry):
        g = step * n_chunks + c
        start(g + nslot - 1)
        slot = g % nslot
        for cp in copies(g, slot):
            cp.wait()
        new = []
        for e in range(eb):
            m, l, acc = carry[e]
            kc = cbuf[slot, e].astype(BF16)
            kr_t = kbuf[slot, e].astype(BF16)
            s = _dot_nt(qs[e][:, :KV_LORA], kc) + _dot(qs[e][:, KV_LORA:KV_LORA + QK_ROPE], kr_t)
            m_new = jnp.maximum(m, jnp.max(s, axis=-1, keepdims=True))
            alpha = jnp.exp2(m - m_new)
            p = jnp.exp2(s - m_new)
            l = alpha * l + jnp.sum(p, axis=-1, keepdims=True)
            acc = alpha * acc + _dot(p.astype(BF16), kc)
            new.append((m_new, l, acc))
        return tuple(new)

    init1 = (jnp.full((MLA_HEADS, 1), -jnp.inf, F32), jnp.zeros((MLA_HEADS, 1), F32),
             jnp.zeros((MLA_HEADS, KV_LORA), F32))
    carry = lax.fori_loop(0, n_chunks, chunk, (init1,) * eb)

    for e in range(eb):
        m, l, acc = carry[e]
        q_lat = qs[e][:, :KV_LORA].astype(F32)
        q_rope = qs[e][:, KV_LORA:KV_LORA + QK_ROPE].astype(F32)
        ckvn = ckvn_ref[e]
        krn = krn_ref[e][:, ROPE_LANE0:ROPE_LANE0 + QK_ROPE]
        s_new = jnp.sum(q_lat * ckvn, axis=-1, keepdims=True) + jnp.sum(q_rope * krn, axis=-1, keepdims=True)
        m_new = jnp.maximum(m, s_new)
        alpha = jnp.exp2(m - m_new)
        p_new = jnp.exp2(s_new - m_new)
        o_ref[e] = (alpha * acc + p_new * ckvn) / (alpha * l + p_new)


def _paged(q_abs, ckv_new, krp_new, cache_ckv, cache_kr, page_table, *, ch, nslot, eb):
    nb, n_pages = page_table.shape
    page = cache_ckv.shape[1]
    assert cache_kr.shape[1:] == (QK_ROPE, page)
    n_chunks = n_pages // ch
    n_steps = nb // eb
    blk = lambda *tail: pl.BlockSpec((eb,) + tail, lambda s, pt: (s,) + (0,) * len(tail))
    grid_spec = pltpu.PrefetchScalarGridSpec(
        num_scalar_prefetch=1, grid=(n_steps,),
        in_specs=[blk(MLA_HEADS, Q_ABS), blk(1, KV_LORA), blk(1, LANE),
                  pl.BlockSpec(memory_space=pl.ANY), pl.BlockSpec(memory_space=pl.ANY)],
        out_specs=blk(MLA_HEADS, KV_LORA),
        scratch_shapes=[pltpu.VMEM((nslot, eb, ch * page, KV_LORA), F32),
                        pltpu.VMEM((nslot, eb, QK_ROPE, ch * page), F32), pltpu.SemaphoreType.DMA((2, nslot))])
    return pl.pallas_call(
        functools.partial(_paged_kernel, n_chunks=n_chunks, ch=ch, page=page, nslot=nslot, n_steps=n_steps, eb=eb),
        grid_spec=grid_spec, out_shape=jax.ShapeDtypeStruct((nb, MLA_HEADS, KV_LORA), F32),
        compiler_params=_cparams(("arbitrary",)), name="paged_attn",
    )(page_table, q_abs, ckv_new, krp_new, cache_ckv, cache_kr)


def _gated_norm(y, z, g):
    return _rms(y * _silu(z), g)


def _ssd_chunk_kernel(xbc_ref, z_ref, dt_ref, cw_ref, cb_ref, dtb_ref, alog_ref, dskip_ref, gn_ref,
                      y_ref, st_ref, xp_sc, *, chunk, eb):
    c = pl.program_id(1)
    halo = 8

    @pl.when(c == 0)
    def _():
        xp_sc[:, 0:halo, :] = jnp.zeros((eb, halo, CONV_CH), F32)
        st_ref[...] = jnp.zeros(st_ref.shape, F32)

    for e in range(eb):
        _ssd_chunk_one(xbc_ref.at[e], z_ref.at[e], dt_ref.at[e], cw_ref, cb_ref, dtb_ref, alog_ref, dskip_ref, gn_ref,
                       y_ref.at[e], st_ref.at[e], xp_sc.at[e], chunk=chunk, halo=halo)


def _ssd_chunk_one(xbc_ref, z_ref, dt_ref, cw_ref, cb_ref, dtb_ref, alog_ref, dskip_ref, gn_ref, y_ref, st_ref, xp_sc,
                   *, chunk, halo):
    xp_sc[halo:halo + chunk, :] = xbc_ref[...]
    conv = cb_ref[...]
    for kk in range(CONV_K):
        off = halo - (CONV_K - 1) + kk
        conv = conv + xp_sc[off:off + chunk, :] * cw_ref[kk:kk + 1, :]
    xp_sc[0:halo, :] = xbc_ref[chunk - halo:chunk, :]
    xbc = _silu(conv)
    xs = xbc[:, :SSM_INNER]
    dt = _softplus(dt_ref[...] + dtb_ref[...])
    da = dt * (-jnp.exp(alog_ref[...]))
    ri = lax.broadcasted_iota(jnp.int32, (chunk, chunk), 0)
    ci = lax.broadcasted_iota(jnp.int32, (chunk, chunk), 1)
    tril = ci <= ri
    cs = jnp.dot(jnp.where(tril, 1.0, 0.0), da, precision=lax.Precision.HIGHEST, preferred_element_type=F32)
    cs_t = cs.T
    dt_t = dt.T
    lane = lax.broadcasted_iota(jnp.int32, (1, LANE), 1)
    rowi = lax.broadcasted_iota(jnp.int32, (LANE, 1), 0)
    lo = lane < SSM_HEADDIM
    heads_per_group = SSM_HEADS // SSM_GROUPS
    ys = []
    for pr in range(SSM_HEADS // 2):
        g = (2 * pr) // heads_per_group
        bm = xbc[:, SSM_INNER + g * D_STATE:SSM_INNER + (g + 1) * D_STATE].astype(BF16)
        cm = xbc[:, SSM_INNER + (SSM_GROUPS + g) * D_STATE:SSM_INNER + (SSM_GROUPS + g + 1) * D_STATE].astype(BF16)
        cb = _dot_nt(cm, bm)
        xp = xs[:, pr * LANE:(pr + 1) * LANE]
        ms, xm, ecols, wcols, dlast, dsk = [], [], [], [], [], []
        for hh in range(2):
            hd = 2 * pr + hh
            col = cs[:, hd:hd + 1]
            rowv = cs_t[hd:hd + 1, :]
            seg = jnp.where(tril, col - rowv, -jnp.inf)
            ms.append((cb * jnp.exp(seg) * dt_t[hd:hd + 1, :]).astype(BF16))
            xm.append(jnp.where(lo if hh == 0 else ~lo, xp, 0.0).astype(BF16))
            last = cs[chunk - 1:chunk, hd:hd + 1]
            ecols.append(jnp.exp(col))
            wcols.append(jnp.exp(last - col) * dt[:, hd:hd + 1])
            dlast.append(jnp.exp(last))
            dsk.append(dskip_ref[:, hd:hd + 1])
        y_diag = _dot(jnp.concatenate(ms, axis=1), jnp.concatenate(xm, axis=0))
        st = st_ref[pr * LANE:(pr + 1) * LANE, :]
        y_off = _dot_nt(cm, st.astype(BF16)) * jnp.where(lo, ecols[0], ecols[1])
        xw = (xp * jnp.where(lo, wcols[0], wcols[1])).astype(BF16)
        st_new = _dot_tn(xw, bm)
        decay = jnp.where(rowi < SSM_HEADDIM, dlast[0], dlast[1])
        st_ref[pr * LANE:(pr + 1) * LANE, :] = decay * st + st_new
        ys.append(y_diag + y_off + jnp.where(lo, dsk[0], dsk[1]) * xp)
    y = jnp.concatenate(ys, axis=1)
    y_ref[...] = _gated_norm(y, z_ref[...], gn_ref[...])


def _ssd_chunk(xbc, z, dt, w, *, batch, seq):
    chunk = SSD_CHUNK if seq % SSD_CHUNK == 0 else seq
    nc = seq // chunk
    t = batch * seq
    eb = 2 if batch % 2 == 0 else 1
    seq_blk = lambda width: pl.BlockSpec((eb, chunk, width), lambda b, c: (b, c, 0))
    y, st = pl.pallas_call(
        functools.partial(_ssd_chunk_kernel, chunk=chunk, eb=eb),
        grid=(batch // eb, nc),
        in_specs=[seq_blk(CONV_CH), seq_blk(SSM_INNER), seq_blk(LANE), _full((CONV_K, CONV_CH)), _full((1, CONV_CH)),
                  _full((1, LANE)), _full((1, LANE)), _full((1, LANE)), _full((1, SSM_INNER))],
        out_specs=[seq_blk(SSM_INNER), pl.BlockSpec((eb, SSM_INNER, D_STATE), lambda b, c: (b, 0, 0))],
        out_shape=[jax.ShapeDtypeStruct((batch, seq, SSM_INNER), F32),
                   jax.ShapeDtypeStruct((batch, SSM_INNER, D_STATE), F32)],
        scratch_shapes=[pltpu.VMEM((eb, 8 + chunk, CONV_CH), F32)],
        compiler_params=_cparams(("parallel", "arbitrary")), name="ssd_chunk",
    )(xbc.reshape(batch, seq, CONV_CH), z.reshape(batch, seq, SSM_INNER), dt.reshape(batch, seq, LANE),
      w['conv_w'], w['conv_b'], w['dt_bias_p'], w['a_log_p'], w['d_skip_p'], w['norm_ssm_out'])
    return y.reshape(t, SSM_INNER), st


def _ssd_step_kernel(xbc_ref, cprev_ref, z_ref, dt_ref, st_ref, cw_ref, cb_ref, dtb_ref, alog_ref, dskip_ref, gn_ref,
                     y_ref, cnew_ref, stn_ref, *, nb):
    xraw = xbc_ref[...]
    conv = cb_ref[...] + xraw * cw_ref[CONV_K - 1:CONV_K, :]
    for kk in range(CONV_K - 1):
        conv = conv + cprev_ref[kk] * cw_ref[kk:kk + 1, :]
    for kk in range(CONV_K - 2):
        cnew_ref[kk] = cprev_ref[kk + 1]
    cnew_ref[CONV_K - 2] = xraw
    xbc = _silu(conv)
    xs = xbc[:, :SSM_INNER]
    dt = _softplus(dt_ref[...] + dtb_ref[...])
    decay = jnp.exp(dt * (-jnp.exp(alog_ref[...])))
    tok_l = lax.broadcasted_iota(jnp.int32, (1, nb), 1)
    heads_per_group = SSM_HEADS // SSM_GROUPS
    xs_t = xs.T
    ys = []
    for hd in range(SSM_HEADS):
        g = hd // heads_per_group
        bm = xbc[:, SSM_INNER + g * D_STATE:SSM_INNER + (g + 1) * D_STATE]
        cm = xbc[:, SSM_INNER + (SSM_GROUPS + g) * D_STATE:SSM_INNER + (SSM_GROUPS + g + 1) * D_STATE]
        x_t = xs_t[hd * SSM_HEADDIM:(hd + 1) * SSM_HEADDIM, :]
        dtb = bm * dt[:, hd:hd + 1]
        y_t = jnp.zeros((SSM_HEADDIM, nb), F32)
        for i in range(nb):
            new = decay[i:i + 1, hd:hd + 1] * st_ref[i, hd] + x_t[:, i:i + 1] * dtb[i:i + 1, :]
            stn_ref[i, hd] = new
            yi = lax.dot_general(new, cm, (((1,), (1,)), ((), ())), precision=lax.Precision.HIGHEST,
                                 preferred_element_type=F32)
            y_t = y_t + jnp.where(tok_l == i, yi, 0.0)
        ys.append(y_t)
    y = jnp.concatenate(ys, axis=0).T
    dsk = jnp.concatenate([jnp.broadcast_to(dskip_ref[:, hd:hd + 1], (1, SSM_HEADDIM)) for hd in range(SSM_HEADS)],
                          axis=1)
    y = y + dsk * xs
    y_ref[...] = _gated_norm(y, z_ref[...], gn_ref[...])


def _ssd_step(xbc, conv_prev, z, dt, state, w, *, nb):
    t = xbc.shape[0]
    row = lambda i: (i, 0)
    return pl.pallas_call(
        functools.partial(_ssd_step_kernel, nb=nb),
        grid=(t // nb,),
        in_specs=[pl.BlockSpec((nb, CONV_CH), row), pl.BlockSpec((CONV_K - 1, nb, CONV_CH), lambda i: (0, i, 0)),
                  pl.BlockSpec((nb, SSM_INNER), row), pl.BlockSpec((nb, LANE), row),
                  pl.BlockSpec((nb, SSM_HEADS, SSM_HEADDIM, D_STATE), lambda i: (i, 0, 0, 0)),
                  _full((CONV_K, CONV_CH)), _full((1, CONV_CH)), _full((1, LANE)), _full((1, LANE)), _full((1, LANE)),
                  _full((1, SSM_INNER))],
        out_specs=[pl.BlockSpec((nb, SSM_INNER), row), pl.BlockSpec((CONV_K - 1, nb, CONV_CH), lambda i: (0, i, 0)),
                   pl.BlockSpec((nb, SSM_HEADS, SSM_HEADDIM, D_STATE), lambda i: (i, 0, 0, 0))],
        out_shape=[jax.ShapeDtypeStruct((t, SSM_INNER), F32), jax.ShapeDtypeStruct((CONV_K - 1, t, CONV_CH), F32),
                   jax.ShapeDtypeStruct((t, SSM_HEADS, SSM_HEADDIM, D_STATE), F32)],
        compiler_params=_cparams(("parallel",)), name="ssd_step",
    )(xbc, conv_prev, z, dt, state, w['conv_w'], w['conv_b'], w['dt_bias_p'], w['a_log_p'], w['d_skip_p'],
      w['norm_ssm_out'])


def _out_proj_kernel(*refs, absorbed):
    refs = refs[:-4] + refs[-3:]
    if absorbed:
        (olat_ref, wuv_ref, yn_ref, x_ref, ga_ref, wout_ref, gffn_ref, wr_ref, br_ref, x1_ref, h_ref, route_ref) = refs
        o_attn = _mm(olat_ref[...], wuv_ref[...])
    else:
        (oat_ref, yn_ref, x_ref, ga_ref, wout_ref, gffn_ref, wr_ref, br_ref, x1_ref, h_ref, route_ref) = refs
        o_attn = oat_ref[...]
    n_attn = MLA_HEADS * V_DIM
    d_model = x_ref.shape[1]
    oa = _rms(o_attn, ga_ref[...])
    mix = _mm(oa, wout_ref[0:n_attn, :]) + _mm(yn_ref[...], wout_ref[n_attn:, :])
    x1 = x_ref[...] + mix
    x1_ref[...] = x1
    h = _rms(x1, gffn_ref[...])
    h_ref[:, 0, 0:d_model] = h
    if wr_ref.dtype == BF16:
        h_hi = h.astype(BF16)
        h_lo = (h - h_hi.astype(F32)).astype(BF16)
        logits = _dot(h_hi, wr_ref[0]) + (_dot(h_lo, wr_ref[0]) + _dot(h_hi, wr_ref[1])) + br_ref[...]
    else:
        logits = jnp.dot(h, wr_ref[0], precision=lax.Precision.HIGHEST, preferred_element_type=F32) + br_ref[...]
    lane = lax.broadcasted_iota(jnp.int32, logits.shape, 1)
    neg = -jnp.inf
    gl = jnp.where(lane < N_GROUPS, logits, neg)
    gmax = jnp.max(gl, axis=-1, keepdims=True)
    g_w = 1.0 / jnp.sum(jnp.exp(gl - gmax), axis=-1, keepdims=True)
    g_idx = jnp.min(jnp.where(gl == gmax, lane, LANE), axis=-1, keepdims=True)
    e_lo = N_GROUPS + EXPERTS_PER_GROUP * g_idx
    el = jnp.where((lane >= e_lo) & (lane < e_lo + EXPERTS_PER_GROUP), logits, neg)
    e1 = jnp.max(el, axis=-1, keepdims=True)
    i1 = jnp.min(jnp.where(el == e1, lane, LANE), axis=-1, keepdims=True)
    el2 = jnp.where(lane == i1, neg, el)
    e2 = jnp.max(el2, axis=-1, keepdims=True)
    i2 = jnp.min(jnp.where(el2 == e2, lane, LANE), axis=-1, keepdims=True)
    r = jnp.exp(e2 - e1)
    c1 = g_w / (1.0 + r)
    c2 = g_w * r / (1.0 + r)
    first = i1 < i2
    ia = jnp.where(first, i1, i2)
    ib = jnp.where(first, i2, i1)
    ca = jnp.where(first, c1, c2)
    cb = jnp.where(first, c2, c1)
    la = ia - e_lo
    lb = ib - e_lo
    pair = jnp.where(la == 0, 0, jnp.where(la == 1, 3, 5)) + (lb - la - 1)
    bucket = g_idx * PAIRS_PER_GROUP + pair
    route = jnp.where(lane == R_EA, (ia - N_GROUPS).astype(F32), 0.0)
    route = jnp.where(lane == R_EB, (ib - N_GROUPS).astype(F32), route)
    route = jnp.where(lane == R_CA, ca, route)
    route = jnp.where(lane == R_CB, cb, route)
    route = jnp.where(lane == R_BUCKET, bucket.astype(F32), route)
    route_ref[...] = route
    h_ref[:, 0, d_model:d_model + LANE] = route


def _out_proj(o_in, yn, x, w, *, absorbed, tm, h_buf, h_row0):
    t, d = x.shape
    row = lambda i: (i, 0)
    assert h_row0 % tm == 0
    h_blk0 = h_row0 // tm
    n_mix = MLA_HEADS * V_DIM + SSM_INNER
    if absorbed:
        first = [pl.BlockSpec((tm, MLA_HEADS * KV_LORA), row), _full((MLA_HEADS * KV_LORA, MLA_HEADS * V_DIM))]
        args = [o_in, w['w_uv_bd32']]
    else:
        first = [pl.BlockSpec((tm, MLA_HEADS * V_DIM), row)]
        args = [o_in]
    w_out = w['w_out32'] if absorbed else w['w_out']
    in_specs = first + [pl.BlockSpec((tm, SSM_INNER), row), pl.BlockSpec((tm, d), row), _full((1, MLA_HEADS * V_DIM)),
                        _full((n_mix, d)), _full((1, d)), _full((2, d, LANE)), _full((1, LANE))]
    args += [yn, x, w['norm_attn_out'], w_out, w['norm_ffn'], w['w_router32' if absorbed else 'w_router'],
             w['b_router']]
    in_specs.append(pl.BlockSpec(memory_space=pl.ANY))
    aliases = {len(args): 1}
    args.append(h_buf)
    return pl.pallas_call(
        functools.partial(_out_proj_kernel, absorbed=absorbed),
        grid=(t // tm,), in_specs=in_specs,
        out_specs=[pl.BlockSpec((tm, d), row), pl.BlockSpec((tm, 1, d + LANE), lambda i: (h_blk0 + i, 0, 0)),
                   pl.BlockSpec((tm, LANE), row)],
        out_shape=[jax.ShapeDtypeStruct((t, d), F32), jax.ShapeDtypeStruct(h_buf.shape, F32),
                   jax.ShapeDtypeStruct((t, LANE), F32)],
        input_output_aliases=aliases,
        compiler_params=_cparams(("parallel",)), name="out_proj_abs" if absorbed else "out_proj",
    )(*args)


def _row_copy(src_hbm, src_row, dst, r, sem):
    return pltpu.make_async_copy(src_hbm.at[src_row], dst.at[pl.ds(r, 1), :], sem)


ROW_GROUP = 32


def _row_groups(n_rows, n_valid, body):
    for r0 in range(0, n_rows, ROW_GROUP):
        def group(r0=r0):
            for r in range(r0, min(r0 + ROW_GROUP, n_rows)):
                body(r)
        if n_valid is None:
            group()
        else:
            pl.when(r0 < n_valid)(group)


def _row_gather(idx_ref, src_hbm, dst, sem, n_rows, n_valid):
    _row_groups(n_rows, n_valid, lambda r: _row_copy(src_hbm, idx_ref[0, 0, r], dst, r, sem).start())


def _row_gather_wait(src_hbm, dst, sem, n_rows, n_valid):
    _row_groups(n_rows, n_valid, lambda r: _row_copy(src_hbm, 0, dst, r, sem).wait())


def _gather_begin(i, idx_ref, src_hbm, buf, sem, n_rows, n_valid=None):
    slot = i % 2

    @pl.when(i == 0)
    def _():
        _row_gather(idx_ref, src_hbm, buf.at[0], sem.at[0], n_rows, n_valid)

    _row_gather_wait(src_hbm, buf.at[slot], sem.at[slot], n_rows, n_valid)
    return slot


def _gather_next(slot, idxn_ref, src_hbm, buf, sem, n_rows, n_valid=None):
    _row_gather(idxn_ref, src_hbm, buf.at[1 - slot], sem.at[1 - slot], n_rows, n_valid)


def _gather_drain(i, n, src_hbm, buf, sem, n_rows, n_valid=None):
    @pl.when(i == n - 1)
    def _():
        other = 1 - i % 2
        _row_gather_wait(src_hbm, buf.at[other], sem.at[other], n_rows, n_valid)


def _moe_kernel(ta_ref, tb_ref, tv_ref, tok_ref, tokn_ref, h_hbm, wga_ref, wua_ref, wda_ref, wgb_ref, wub_ref,
                wdb_ref, y_ref, xbuf, sem, *, tm):
    del ta_ref, tb_ref
    i = pl.program_id(0)
    n = pl.num_programs(0)
    n_valid = tv_ref[i]
    n_valid_next = tv_ref[jnp.minimum(i + 1, n - 1)]

    @pl.when(i == 0)
    def _():
        xbuf[...] = jnp.zeros(xbuf.shape, F32)

    slot = _gather_begin(i, tok_ref, h_hbm, xbuf, sem, tm, n_valid)
    _gather_next(slot, tokn_ref, h_hbm, xbuf, sem, tm, n_valid_next)
    d = y_ref.shape[2]

    @pl.when(n_valid > 0)
    def _():
        x = xbuf[slot, :, 0:d].astype(BF16)
        rec = xbuf[slot, :, d:d + LANE]
        y = None
        for (wg_ref, wu_ref, wd_ref), lane_c in (((wga_ref, wua_ref, wda_ref), R_CA),
                                                 ((wgb_ref, wub_ref, wdb_ref), R_CB)):
            act = _silu(_dot(x, wg_ref[0])) * _dot(x, wu_ref[0])
            term = rec[:, lane_c:lane_c + 1] * _dot(act.astype(BF16), wd_ref[0])
            y = term if y is None else y + term
        y_ref[:, 0, :] = y

    @pl.when(n_valid == 0)
    def _():
        y_ref[...] = jnp.zeros(y_ref.shape, F32)

    _gather_drain(i, n, h_hbm, xbuf, sem, tm, n_valid_next)


def _moe(h, tile_ea, tile_eb, tile_valid, row_token, w, *, tm):
    t, _, dp = h.shape
    d = dp - LANE
    n_tiles = tile_ea.shape[0]
    ea = lambda i, ta, tb, tv: (ta[i], 0, 0)
    eb = lambda i, ta, tb, tv: (tb[i], 0, 0)
    grid_spec = pltpu.PrefetchScalarGridSpec(
        num_scalar_prefetch=3, grid=(n_tiles,),
        in_specs=[pl.BlockSpec((1, 1, tm), lambda i, ta, tb, tv: (i, 0, 0), memory_space=pltpu.SMEM),
                  pl.BlockSpec((1, 1, tm), lambda i, ta, tb, tv: (jnp.minimum(i + 1, n_tiles - 1), 0, 0),
                               memory_space=pltpu.SMEM),
                  pl.BlockSpec(memory_space=pl.ANY),
                  pl.BlockSpec((1, d, EXPERT_FF), ea), pl.BlockSpec((1, d, EXPERT_FF), ea),
                  pl.BlockSpec((1, EXPERT_FF, d), ea),
                  pl.BlockSpec((1, d, EXPERT_FF), eb), pl.BlockSpec((1, d, EXPERT_FF), eb),
                  pl.BlockSpec((1, EXPERT_FF, d), eb)],
        out_specs=pl.BlockSpec((tm, 1, d), lambda i, ta, tb, tv: (i, 0, 0)),
        scratch_shapes=[pltpu.VMEM((2, tm, dp), F32), pltpu.SemaphoreType.DMA((2,))])
    rt = row_token.reshape(n_tiles, 1, tm)
    wts = (w['w_exp_gate'], w['w_exp_up'], w['w_exp_down'])
    return pl.pallas_call(
        functools.partial(_moe_kernel, tm=tm), grid_spec=grid_spec,
        out_shape=jax.ShapeDtypeStruct((n_tiles * tm, 1, d), F32),
        compiler_params=_cparams(("arbitrary",)), name="moe",
    )(tile_ea, tile_eb, tile_valid, rt, rt, h, *wts, *wts)


def _moe_plan(route, *, tm):
    t = route.shape[0]
    bids = route[:, R_BUCKET].astype(jnp.int32)
    onehot = (bids[:, None] == jnp.arange(N_BUCKETS, dtype=jnp.int32)[None, :]).astype(F32)
    blk = _tile(t, LANE)
    oh = onehot.reshape(-1, blk, N_BUCKETS)
    intra = jnp.einsum('ts,bse->bte', jnp.tril(jnp.ones((blk, blk), F32)), oh)
    tot = intra[:, -1, :]
    csum = (intra + (jnp.cumsum(tot, axis=0) - tot)[:, None, :]).reshape(-1, N_BUCKETS)
    rank = jnp.sum(onehot * (csum - 1.0), axis=1).astype(jnp.int32)
    counts = csum[-1].astype(jnp.int32)
    padded = ((counts + tm - 1) // tm) * tm
    ends = jnp.cumsum(padded)
    offs = ends - padded
    pos = offs[bids] + rank
    n_tiles = -(-t // tm) + N_BUCKETS
    row_token = jnp.zeros((n_tiles * tm,), jnp.int32).at[pos].set(jnp.arange(t, dtype=jnp.int32))
    tile_start = jnp.arange(n_tiles, dtype=jnp.int32) * tm
    tile_bucket = jnp.sum((ends[None, :] <= tile_start[:, None]).astype(jnp.int32), axis=1)
    tile_bucket = jnp.minimum(tile_bucket, N_BUCKETS - 1)
    group0 = (tile_bucket // PAIRS_PER_GROUP) * EXPERTS_PER_GROUP
    pair = tile_bucket % PAIRS_PER_GROUP
    tile_ea = group0 + jnp.asarray(PAIR_A, jnp.int32)[pair]
    tile_eb = group0 + jnp.asarray(PAIR_B, jnp.int32)[pair]
    tile_valid = jnp.clip((offs + counts)[tile_bucket] - tile_start, 0, tm).astype(jnp.int32)
    return tile_ea, tile_eb, tile_valid, row_token, pos


def _final_kernel(pos_ref, posn_ref, x1_ref, p_ref, y_hbm, gple_ref, wgate_ref, wproj_ref, gfin_ref,
                  out_ref, ybuf, sem, *, tm):
    i = pl.program_id(0)
    n = pl.num_programs(0)
    slot = _gather_begin(i, pos_ref, y_hbm, ybuf, sem, tm)
    _gather_next(slot, posn_ref, y_hbm, ybuf, sem, tm)
    x2 = x1_ref[...] + ybuf[slot]
    gate_in = _rms(x2, gple_ref[...]).astype(BF16)
    gate = 1.0 / (1.0 + jnp.exp(-_dot(gate_in, wgate_ref[...])))
    x3 = x2 + gate * _dot(p_ref[...].astype(BF16), wproj_ref[...])
    out_ref[...] = _rms(x3, gfin_ref[...])
    _gather_drain(i, n, y_hbm, ybuf, sem, tm)


def _final(x1, p, y_sorted, pos, w, *, tm):
    t, d = x1.shape
    nt = t // tm
    ple = p.shape[1]
    pos_t = pos.reshape(nt, 1, tm)
    row = lambda i: (i, 0)
    in_specs = [pl.BlockSpec((1, 1, tm), lambda i: (i, 0, 0), memory_space=pltpu.SMEM),
                pl.BlockSpec((1, 1, tm), lambda i: (jnp.minimum(i + 1, nt - 1), 0, 0), memory_space=pltpu.SMEM),
                pl.BlockSpec((tm, d), row), pl.BlockSpec((tm, ple), row),
                pl.BlockSpec(memory_space=pl.ANY), _full((1, d)), _full((d, d)), _full((ple, d)), _full((1, d))]
    return pl.pallas_call(
        functools.partial(_final_kernel, tm=tm), grid=(nt,), in_specs=in_specs,
        out_specs=pl.BlockSpec((tm, d), row), out_shape=jax.ShapeDtypeStruct((t, d), F32),
        scratch_shapes=[pltpu.VMEM((2, tm, d), F32), pltpu.SemaphoreType.DMA((2,))],
        compiler_params=_cparams(("arbitrary",)), name="final",
    )(pos_t, pos_t, x1, p, y_sorted, w['norm_ple'], w['w_ple_gate'], w['w_ple_proj'], w['norm_final'])


def _pad_lanes(v, width=LANE):
    v = v.reshape(1, -1)
    return jnp.pad(v, ((0, 0), (0, width - v.shape[1])))


def _rot_cols(wr):
    half = QK_ROPE // 2
    return jnp.concatenate([-wr[..., half:], wr[..., :half]], axis=-1)


def _prep_weights(norm_mix, w_in, norm_q, w_uq, norm_kv, w_ukv, norm_attn_out, conv_w, conv_b, dt_bias, a_log, d_skip,
                  norm_ssm_out, w_out, norm_ffn, w_group_router, b_group_router, w_expert_router, b_expert_router,
                  w_exp_gate, w_exp_up, w_exp_down, norm_ple, w_ple_gate, w_ple_proj, norm_final):
    d = w_in.shape[0]
    splits = [Q_LORA, KV_LORA, QK_ROPE, SSM_INNER, CONV_CH, SSM_HEADS]
    bounds = [0]
    for s in splits:
        bounds.append(bounds[-1] + s)
    w_cq, w_ckv, w_kr, w_z, w_xbc, w_dt = [w_in[:, bounds[j]:bounds[j + 1]] for j in range(6)]
    zpad = lambda n: jnp.zeros((d, n), F32)
    place = lambda m: jnp.concatenate([zpad(ROPE_LANE0), m, zpad(LANE - ROPE_LANE0 - QK_ROPE)], axis=1)
    w_in_p = jnp.concatenate([w_cq, w_ckv, place(w_kr), place(_rot_cols(w_kr)), w_z, w_xbc, w_dt,
                              zpad(LANE - SSM_HEADS)], axis=1)
    nope, rp = w_uq[..., :QK_NOPE], w_uq[..., QK_NOPE:]
    zq = lambda n: jnp.zeros((Q_LORA, MLA_HEADS, n), F32)
    tail = HEAD_PAD - QK_NOPE - QK_ROPE
    wq_a = jnp.concatenate([nope, rp, zq(tail)], axis=-1).reshape(Q_LORA, -1)
    wq_b = jnp.concatenate([zq(QK_NOPE), _rot_cols(rp), zq(tail)], axis=-1).reshape(Q_LORA, -1)
    w_uk, w_uv = w_ukv[..., :QK_NOPE], w_ukv[..., QK_NOPE:]
    wk_p = jnp.concatenate([w_uk, jnp.zeros((KV_LORA, MLA_HEADS, HEAD_PAD - QK_NOPE), F32)], axis=-1)
    w_kv_p = jnp.concatenate([wk_p.reshape(KV_LORA, -1), w_uv.reshape(KV_LORA, -1)], axis=1).astype(BF16)
    w_abs = jnp.zeros((MLA_HEADS, HEAD_PAD, Q_ABS), F32)
    w_abs = w_abs.at[:, :QK_NOPE, :KV_LORA].set(jnp.transpose(w_uk, (1, 2, 0)))
    w_abs = w_abs.at[:, QK_NOPE:QK_NOPE + QK_ROPE, KV_LORA:KV_LORA + QK_ROPE].set(
        jnp.broadcast_to(jnp.eye(QK_ROPE, dtype=F32), (MLA_HEADS, QK_ROPE, QK_ROPE)))
    w_uv_bd = jnp.zeros((MLA_HEADS, KV_LORA, MLA_HEADS, V_DIM), F32)
    hidx = jnp.arange(MLA_HEADS)
    w_uv_bd = w_uv_bd.at[hidx, :, hidx, :].set(jnp.transpose(w_uv, (1, 0, 2)))
    w_router = jnp.concatenate([w_group_router, w_expert_router,
                                jnp.zeros((d, LANE - N_GROUPS - N_EXPERTS), F32)], axis=1)
    b_router = _pad_lanes(jnp.concatenate([b_group_router, b_expert_router]))
    w_router_hi = w_router.astype(BF16)
    w_router_lo = (w_router - w_router_hi.astype(F32)).astype(BF16)
    w_router32 = jnp.stack([w_router, jnp.zeros_like(w_router)])
    w_router = jnp.stack([w_router_hi, w_router_lo])
    return dict(
        norm_mix=norm_mix.reshape(1, -1), w_in_p=w_in_p.astype(BF16), w_in_p32=w_in_p, norm_q=norm_q.reshape(1, -1),
        wq_a=wq_a.astype(BF16), wq_b=wq_b.astype(BF16), wq_a32=wq_a, wq_b32=wq_b,
        norm_kv=norm_kv.reshape(1, -1), w_kv_p=w_kv_p, w_abs32=w_abs,
        w_uv_bd32=w_uv_bd.reshape(MLA_HEADS * KV_LORA, MLA_HEADS * V_DIM),
        norm_attn_out=norm_attn_out.reshape(1, -1), conv_w=conv_w, conv_b=conv_b.reshape(1, -1),
        dt_bias_p=_pad_lanes(dt_bias), a_log_p=_pad_lanes(a_log), d_skip_p=_pad_lanes(d_skip),
        norm_ssm_out=norm_ssm_out.reshape(1, -1), w_out=w_out.astype(BF16), w_out32=w_out,
        norm_ffn=norm_ffn.reshape(1, -1),
        w_router=w_router, w_router32=w_router32, b_router=b_router, w_exp_gate=w_exp_gate.astype(BF16), w_exp_up=w_exp_up.astype(BF16),
        w_exp_down=w_exp_down.astype(BF16), norm_ple=norm_ple.reshape(1, -1), w_ple_gate=w_ple_gate.astype(BF16),
        w_ple_proj=w_ple_proj.astype(BF16), norm_final=norm_final.reshape(1, -1))


def _rope_tables(pos):
    half = QK_ROPE // 2
    inv = ROPE_THETA ** (-jnp.arange(half, dtype=F32) / half)
    ang = pos.astype(F32)[:, None] * inv[None, :]
    place = lambda m: jnp.pad(jnp.concatenate([m, m], axis=1), ((0, 0), (ROPE_LANE0, LANE - ROPE_LANE0 - QK_ROPE)))
    return place(jnp.cos(ang)), place(jnp.sin(ang))


def _tile(n, pref):
    return pref if n % pref == 0 else n


def kernel(x_prompt, x_sample, p_prompt, p_sample, cache_ckv, cache_krope, state_conv, state_ssm, page_table, norm_mix, w_in, norm_q, w_uq, norm_kv, w_ukv, norm_attn_out, conv_w, conv_b, dt_bias, a_log, d_skip, norm_ssm_out, w_out, norm_ffn, w_group_router, b_group_router, w_expert_router, b_expert_router, w_exp_gate, w_exp_up, w_exp_down, norm_ple, w_ple_gate, w_ple_proj, norm_final):
    depth = w_in.shape[0]
    assert depth == 1, "single-layer stack"
    bp, sp, d = x_prompt.shape
    bs, ls, _ = x_sample.shape
    assert ls == 1, "decode step handles one new token per sequence"
    n_pool, page, _ = cache_ckv.shape[1:]
    n_pages = page_table.shape[1]
    past_len = n_pages * page
    w = _prep_weights(norm_mix[0], w_in[0], norm_q[0], w_uq[0], norm_kv[0], w_ukv[0], norm_attn_out[0], conv_w[0],
                      conv_b[0], dt_bias[0], a_log[0], d_skip[0], norm_ssm_out[0], w_out[0], norm_ffn[0],
                      w_group_router[0], b_group_router[0], w_expert_router[0], b_expert_router[0], w_exp_gate[0],
                      w_exp_up[0], w_exp_down[0], norm_ple[0], w_ple_gate[0], w_ple_proj[0], norm_final)

    tp = bp * sp
    xp = x_prompt.reshape(tp, d)
    tm_p = _tile(sp, 512)
    cos_p, sin_p = _rope_tables(jnp.arange(sp))
    q, k, v, ckv_p, krp_p, z_p, xbc_p, dt_p = _in_proj(xp, cos_p, sin_p, w, absorb=False, tm=tm_p,
                                                       n_pos_blocks=sp // tm_p)
    tq = _tile(sp, 1024)
    o_attn = _flash(q, k, v, batch=bp, seq=sp, tq=tq, tk=tq)
    yn_p, ssm_p = _ssd_chunk(xbc_p, z_p, dt_p, w, batch=bp, seq=sp)
    h_all = jnp.zeros((tp + bs, 1, d + LANE), F32)
    x1_p, h_all, route_p = _out_proj(o_attn, yn_p, xp, w, absorbed=False, tm=_tile(tp, 512), h_buf=h_all, h_row0=0)

    xs = x_sample.reshape(bs, d)
    cos_s, sin_s = _rope_tables(jnp.full((bs,), past_len, jnp.int32))
    q_abs, ckv_s, krp_s, z_s, xbc_s, dt_s = _in_proj(xs, cos_s, sin_s, w, absorb=True, tm=bs, n_pos_blocks=1)
    ch = 16 if n_pages % 16 == 0 else n_pages
    o_lat = _paged(q_abs.reshape(bs, MLA_HEADS, Q_ABS), ckv_s.reshape(bs, 1, KV_LORA), krp_s.reshape(bs, 1, LANE),
                   cache_ckv.reshape(n_pool, page, KV_LORA), jnp.swapaxes(cache_krope, 2, 3).reshape(n_pool, QK_ROPE, page), page_table,
                   ch=ch, nslot=3, eb=2 if bs % 2 == 0 else 1)
    conv_prev = jnp.transpose(state_conv[0], (1, 0, 2))
    yn_s, conv_new, ssm_s = _ssd_step(xbc_s, conv_prev, z_s, dt_s, state_ssm[0], w, nb=_tile(bs, 8))
    x1_s, h_all, route_s = _out_proj(o_lat.reshape(bs, MLA_HEADS * KV_LORA), yn_s, xs, w, absorbed=True, tm=bs,
                                     h_buf=h_all, h_row0=tp)

    tm_moe = 256
    tile_ea, tile_eb, tile_valid, row_token, pos = _moe_plan(jnp.concatenate([route_p, route_s], axis=0), tm=tm_moe)
    y_sorted = _moe(h_all, tile_ea, tile_eb, tile_valid, row_token, w, tm=tm_moe)
    y_prompt = _final(x1_p, p_prompt[0].reshape(tp, -1), y_sorted, pos[:tp], w, tm=_tile(tp, 512))
    y_sample = _final(x1_s, p_sample[0].reshape(bs, -1), y_sorted, pos[tp:], w, tm=bs)

    rope_sl = slice(ROPE_LANE0, ROPE_LANE0 + QK_ROPE)
    return (y_prompt.reshape(bp, sp, d), y_sample.reshape(bs, ls, d),
            ckv_p.reshape(1, bp, sp, KV_LORA), krp_p[:, rope_sl].reshape(1, bp, sp, QK_ROPE),
            xbc_p.reshape(bp, sp, CONV_CH)[:, sp - (CONV_K - 1):, :][None],
            ssm_p.reshape(1, bp, SSM_HEADS, SSM_HEADDIM, D_STATE),
            ckv_s.reshape(1, bs, ls, KV_LORA), krp_s[:, rope_sl].reshape(1, bs, ls, QK_ROPE),
            jnp.transpose(conv_new, (1, 0, 2))[None], ssm_s[None])
```

```python
import functools
import math

import jax
import jax.numpy as jnp
from jax import lax
from jax.experimental import pallas as pl
from jax.experimental.pallas import tpu as pltpu

F32 = jnp.float32
BF16 = jnp.bfloat16

MLA_HEADS = 8
QK_NOPE = 64
QK_ROPE = 32
V_DIM = 64
Q_LORA = 384
KV_LORA = 256
ROPE_THETA = 10000.0
ATTN_SCALE = 1.0 / math.sqrt(QK_NOPE + QK_ROPE)
LOG2E = math.log2(math.e)
Q_SCALE = ATTN_SCALE * LOG2E
SSM_HEADS = 8
SSM_HEADDIM = 64
SSM_INNER = SSM_HEADS * SSM_HEADDIM
SSM_GROUPS = 2
D_STATE = 128
CONV_K = 4
CONV_CH = SSM_INNER + 2 * SSM_GROUPS * D_STATE
SSD_CHUNK = 128
N_GROUPS = 4
EXPERTS_PER_GROUP = 4
N_EXPERTS = N_GROUPS * EXPERTS_PER_GROUP
EXPERT_FF = 512
EPS = 1e-6
PAIR_A = (0, 0, 0, 1, 1, 2)
PAIR_B = (1, 2, 3, 2, 3, 3)
PAIRS_PER_GROUP = len(PAIR_A)
N_BUCKETS = N_GROUPS * PAIRS_PER_GROUP
R_EA, R_EB, R_CA, R_CB, R_BUCKET = 0, 1, 2, 3, 4

LANE = 128
HEAD_PAD = 128
ROPE_LANE0 = QK_NOPE
Q_ABS = KV_LORA + LANE

C_CQ = 0
C_CKV = C_CQ + Q_LORA
C_KRA = C_CKV + KV_LORA
C_KRB = C_KRA + LANE
C_Z = C_KRB + LANE
C_XBC = C_Z + SSM_INNER
C_DT = C_XBC + CONV_CH
IN_COLS_P = C_DT + LANE

VMEM_LIMIT = 56 * 1024 * 1024


def _cparams(sem, vmem=VMEM_LIMIT):
    return pltpu.CompilerParams(dimension_semantics=sem, vmem_limit_bytes=vmem)


def _rms(x, g):
    return x * lax.rsqrt(jnp.mean(x * x, axis=-1, keepdims=True) + EPS) * g


def _dot(a, b):
    return jnp.dot(a, b, preferred_element_type=F32)


def _mm(a, w):
    if w.dtype == BF16:
        return jnp.dot(a.astype(BF16), w, preferred_element_type=F32)
    return jnp.dot(a, w, precision=lax.Precision.HIGHEST, preferred_element_type=F32)


def _dot_nt(a, b):
    return lax.dot_general(a, b, (((1,), (1,)), ((), ())), preferred_element_type=F32)


def _dot_tn(a, b):
    return lax.dot_general(a, b, (((0,), (0,)), ((), ())), preferred_element_type=F32)


def _silu(x):
    return x * (1.0 / (1.0 + jnp.exp(-x)))


def _softplus(x):
    return jnp.maximum(x, 0.0) + jnp.log(1.0 + jnp.exp(-jnp.abs(x)))


def _full(shape):
    nd = len(shape)
    return pl.BlockSpec(shape, lambda *a: (0,) * nd)


def _in_proj_kernel(x_ref, gmix_ref, win_ref, gq_ref, wqa_ref, wqb_ref, gkv_ref, wkv_ref, cos_ref, sin_ref,
                    *refs, absorb):
    if absorb:
        wabs_ref, q_ref, ckv_ref, krp_ref, z_ref, xbc_ref, dt_ref = refs
    else:
        q_ref, k_ref, v_ref, ckv_ref, krp_ref, z_ref, xbc_ref, dt_ref = refs
    h = _rms(x_ref[...], gmix_ref[...])
    u = _mm(h, win_ref[...])
    cq = _rms(u[:, C_CQ:C_CQ + Q_LORA], gq_ref[...])
    qa = _mm(cq, wqa_ref[...])
    qb = _mm(cq, wqb_ref[...])
    cos = cos_ref[...]
    sin = sin_ref[...]
    lane = lax.broadcasted_iota(jnp.int32, (1, LANE), 1)
    q_cos = Q_SCALE * (jnp.where(lane < QK_NOPE, 1.0, 0.0) + cos)
    q_sin = Q_SCALE * sin
    ckv = _rms(u[:, C_CKV:C_CKV + KV_LORA], gkv_ref[...])
    ckv_ref[...] = ckv
    krp = u[:, C_KRA:C_KRA + LANE] * cos + u[:, C_KRB:C_KRB + LANE] * sin
    krp_ref[...] = krp
    z_ref[...] = u[:, C_Z:C_Z + SSM_INNER]
    xbc_ref[...] = u[:, C_XBC:C_XBC + CONV_CH]
    dt_ref[...] = u[:, C_DT:C_DT + LANE]
    for hd in range(MLA_HEADS):
        sl = slice(hd * HEAD_PAD, (hd + 1) * HEAD_PAD)
        qh = qa[:, sl] * q_cos + qb[:, sl] * q_sin
        if absorb:
            q_ref[:, hd * Q_ABS:(hd + 1) * Q_ABS] = _mm(qh, wabs_ref[hd]).astype(BF16)
        else:
            q_ref[:, sl] = qh.astype(BF16)
    if not absorb:
        kv = _mm(ckv, wkv_ref[...])
        for hd in range(MLA_HEADS):
            sl = slice(hd * HEAD_PAD, (hd + 1) * HEAD_PAD)
            k_ref[:, sl] = (kv[:, sl] + krp).astype(BF16)
        v_ref[...] = kv[:, MLA_HEADS * HEAD_PAD:].astype(BF16)


def _in_proj(x, cos, sin, w, *, absorb, tm, n_pos_blocks):
    t, d = x.shape
    nt = t // tm
    row = lambda i: (i, 0)
    pos = lambda i: (i % n_pos_blocks, 0)
    in_specs = [
        pl.BlockSpec((tm, d), row), _full((1, d)), _full((d, IN_COLS_P)), _full((1, Q_LORA)),
        _full((Q_LORA, MLA_HEADS * HEAD_PAD)), _full((Q_LORA, MLA_HEADS * HEAD_PAD)), _full((1, KV_LORA)),
        _full((KV_LORA, MLA_HEADS * (HEAD_PAD + V_DIM))), pl.BlockSpec((tm, LANE), pos), pl.BlockSpec((tm, LANE), pos),
    ]
    sfx = '32' if absorb else ''
    args = [x, w['norm_mix'], w['w_in_p' + sfx], w['norm_q'], w['wq_a' + sfx], w['wq_b' + sfx], w['norm_kv'],
            w['w_kv_p'], cos, sin]
    tail_shapes = [
        jax.ShapeDtypeStruct((t, KV_LORA), F32), jax.ShapeDtypeStruct((t, LANE), F32),
        jax.ShapeDtypeStruct((t, SSM_INNER), F32), jax.ShapeDtypeStruct((t, CONV_CH), F32),
        jax.ShapeDtypeStruct((t, LANE), F32),
    ]
    tail_specs = [pl.BlockSpec((tm, KV_LORA), row), pl.BlockSpec((tm, LANE), row), pl.BlockSpec((tm, SSM_INNER), row),
                  pl.BlockSpec((tm, CONV_CH), row), pl.BlockSpec((tm, LANE), row)]
    if absorb:
        in_specs.append(_full((MLA_HEADS, HEAD_PAD, Q_ABS)))
        args.append(w['w_abs32'])
        out_shape = [jax.ShapeDtypeStruct((t, MLA_HEADS * Q_ABS), BF16)] + tail_shapes
        out_specs = [pl.BlockSpec((tm, MLA_HEADS * Q_ABS), row)] + tail_specs
    else:
        out_shape = [jax.ShapeDtypeStruct((t, MLA_HEADS * HEAD_PAD), BF16),
                     jax.ShapeDtypeStruct((t, MLA_HEADS * HEAD_PAD), BF16),
                     jax.ShapeDtypeStruct((t, MLA_HEADS * V_DIM), BF16)] + tail_shapes
        out_specs = [pl.BlockSpec((tm, MLA_HEADS * HEAD_PAD), row), pl.BlockSpec((tm, MLA_HEADS * HEAD_PAD), row),
                     pl.BlockSpec((tm, MLA_HEADS * V_DIM), row)] + tail_specs
    return pl.pallas_call(
        functools.partial(_in_proj_kernel, absorb=absorb),
        grid=(nt,), in_specs=in_specs, out_specs=out_specs, out_shape=out_shape,
        compiler_params=_cparams(("parallel",)), name="in_proj_abs" if absorb else "in_proj",
    )(*args)


def _flash_kernel(q_ref, k_ref, v_ref, o_ref, *, tq, tk):
    qi = pl.program_id(2)
    lane = lax.broadcasted_iota(jnp.int32, (1, LANE), 1)
    qs = [q_ref[:, hh * HEAD_PAD:(hh + 1) * HEAD_PAD] for hh in range(2)]

    def step(ki, carry, masked):
        ks = pl.multiple_of(ki * tk, tk)
        v = v_ref[pl.ds(ks, tk), :]
        if masked:
            row = qi * tq + lax.broadcasted_iota(jnp.int32, (tq, tk), 0)
            col = ks + lax.broadcasted_iota(jnp.int32, (tq, tk), 1)
            keep = col <= row
        out = []
        for hh in range(2):
            m, l, acc = carry[hh]
            s = _dot_nt(qs[hh], k_ref[pl.ds(ks, tk), hh * HEAD_PAD:(hh + 1) * HEAD_PAD])
            if masked:
                s = jnp.where(keep, s, -jnp.inf)
            m_new = jnp.maximum(m, jnp.max(s, axis=-1, keepdims=True))
            alpha = jnp.exp2(m - m_new)
            p = jnp.exp2(s - m_new)
            l = alpha * l + jnp.sum(p, axis=-1, keepdims=True)
            acc = alpha * acc + _dot(p.astype(BF16), v)
            out.append((m_new, l, acc))
        return tuple(out)

    init1 = (jnp.full((tq, 1), -jnp.inf, F32), jnp.zeros((tq, 1), F32), jnp.zeros((tq, 2 * V_DIM), F32))
    n_full = (qi * tq) // tk
    carry = lax.fori_loop(0, n_full, functools.partial(step, masked=False), (init1, init1))
    n_masked = (tq + tk - 1) // tk
    for d in range(n_masked):
        carry = step(n_full + d, carry, True)
    o0 = carry[0][2] / carry[0][1]
    o1 = carry[1][2] / carry[1][1]
    o_ref[...] = jnp.where(lane < V_DIM, o0, o1)


def _flash(q, k, v, *, batch, seq, tq, tk):
    t = batch * seq
    nq = seq // tq
    npair = MLA_HEADS // 2
    return pl.pallas_call(
        functools.partial(_flash_kernel, tq=tq, tk=tk),
        grid=(batch, npair, nq),
        in_specs=[pl.BlockSpec((tq, 2 * HEAD_PAD), lambda b, hp, i: (b * nq + i, hp)),
                  pl.BlockSpec((seq, 2 * HEAD_PAD), lambda b, hp, i: (b, hp)),
                  pl.BlockSpec((seq, 2 * V_DIM), lambda b, hp, i: (b, hp))],
        out_specs=pl.BlockSpec((tq, 2 * V_DIM), lambda b, hp, i: (b * nq + i, hp)),
        out_shape=jax.ShapeDtypeStruct((t, MLA_HEADS * V_DIM), F32),
        compiler_params=_cparams(("parallel", "parallel", "arbitrary")), name="flash",
    )(q, k, v)


def _paged_kernel(pt_ref, q_ref, ckvn_ref, krn_ref, cc_hbm, ck_hbm, o_ref, cbuf, kbuf, sem,
                  *, n_chunks, ch, page, nslot, n_steps, eb):
    step = pl.program_id(0)
    total = n_steps * n_chunks

    def copies(g, slot):
        gs = g // n_chunks
        gc = g % n_chunks
        out = []
        for e in range(eb):
            for j in range(ch):
                pg = pt_ref[gs * eb + e, gc * ch + j]
                out.append(pltpu.make_async_copy(cc_hbm.at[pg], cbuf.at[slot, e, pl.ds(j * page, page), :],
                                                 sem.at[0, slot]))
                out.append(pltpu.make_async_copy(ck_hbm.at[pg], kbuf.at[slot, e, :, pl.ds(j * page, page)],
                                                 sem.at[1, slot]))
        return out

    def start(g):
        @pl.when(g < total)
        def _():
            for c in copies(g, g % nslot):
                c.start()

    @pl.when(step == 0)
    def _():
        for g0 in range(nslot - 1):
            start(jnp.int32(g0))

    qs = [q_ref[e] for e in range(eb)]

    def chunk(c, carry):
        g = step * n_chunks + c
        start(g + nslot - 1)
        slot = g % nslot
        for cp in copies(g, slot):
            cp.wait()
        new = []
        for e in range(eb):
            m, l, acc = carry[e]
            kc = cbuf[slot, e].astype(BF16)
            kr_t = kbuf[slot, e].astype(BF16)
            s = _dot_nt(qs[e][:, :KV_LORA], kc) + _dot(qs[e][:, KV_LORA:KV_LORA + QK_ROPE], kr_t)
            m_new = jnp.maximum(m, jnp.max(s, axis=-1, keepdims=True))
            alpha = jnp.exp2(m - m_new)
            p = jnp.exp2(s - m_new)
            l = alpha * l + jnp.sum(p, axis=-1, keepdims=True)
            acc = alpha * acc + _dot(p.astype(BF16), kc)
            new.append((m_new, l, acc))
        return tuple(new)

    init1 = (jnp.full((MLA_HEADS, 1), -jnp.inf, F32), jnp.zeros((MLA_HEADS, 1), F32),
             jnp.zeros((MLA_HEADS, KV_LORA), F32))
    carry = lax.fori_loop(0, n_chunks, chunk, (init1,) * eb)

    for e in range(eb):
        m, l, acc = carry[e]
        q_lat = qs[e][:, :KV_LORA].astype(F32)
        q_rope = qs[e][:, KV_LORA:KV_LORA + QK_ROPE].astype(F32)
        ckvn = ckvn_ref[e]
        krn = krn_ref[e][:, ROPE_LANE0:ROPE_LANE0 + QK_ROPE]
        s_new = jnp.sum(q_lat * ckvn, axis=-1, keepdims=True) + jnp.sum(q_rope * krn, axis=-1, keepdims=True)
        m_new = jnp.maximum(m, s_new)
        alpha = jnp.exp2(m - m_new)
        p_new = jnp.exp2(s_new - m_new)
        o_ref[e] = (alpha * acc + p_new * ckvn) / (alpha * l + p_new)


def _paged(q_abs, ckv_new, krp_new, cache_ckv, cache_kr, page_table, *, ch, nslot, eb):
    nb, n_pages = page_table.shape
    page = cache_ckv.shape[1]
    assert cache_kr.shape[1:] == (QK_ROPE, page)
    n_chunks = n_pages // ch
    n_steps = nb // eb
    blk = lambda *tail: pl.BlockSpec((eb,) + tail, lambda s, pt: (s,) + (0,) * len(tail))
    grid_spec = pltpu.PrefetchScalarGridSpec(
        num_scalar_prefetch=1, grid=(n_steps,),
        in_specs=[blk(MLA_HEADS, Q_ABS), blk(1, KV_LORA), blk(1, LANE),
                  pl.BlockSpec(memory_space=pl.ANY), pl.BlockSpec(memory_space=pl.ANY)],
        out_specs=blk(MLA_HEADS, KV_LORA),
        scratch_shapes=[pltpu.VMEM((nslot, eb, ch * page, KV_LORA), F32),
                        pltpu.VMEM((nslot, eb, QK_ROPE, ch * page), F32), pltpu.SemaphoreType.DMA((2, nslot))])
    return pl.pallas_call(
        functools.partial(_paged_kernel, n_chunks=n_chunks, ch=ch, page=page, nslot=nslot, n_steps=n_steps, eb=eb),
        grid_spec=grid_spec, out_shape=jax.ShapeDtypeStruct((nb, MLA_HEADS, KV_LORA), F32),
        compiler_params=_cparams(("arbitrary",)), name="paged_attn",
    )(page_table, q_abs, ckv_new, krp_new, cache_ckv, cache_kr)


def _gated_norm(y, z, g):
    return _rms(y * _silu(z), g)


def _ssd_chunk_kernel(xbc_ref, z_ref, dt_ref, cw_ref, cb_ref, dtb_ref, alog_ref, dskip_ref, gn_ref,
                      y_ref, st_ref, xp_sc, *, chunk, eb):
    c = pl.program_id(1)
    halo = 8

    @pl.when(c == 0)
    def _():
        xp_sc[:, 0:halo, :] = jnp.zeros((eb, halo, CONV_CH), F32)
        st_ref[...] = jnp.zeros(st_ref.shape, F32)

    for e in range(eb):
        _ssd_chunk_one(xbc_ref.at[e], z_ref.at[e], dt_ref.at[e], cw_ref, cb_ref, dtb_ref, alog_ref, dskip_ref, gn_ref,
                       y_ref.at[e], st_ref.at[e], xp_sc.at[e], chunk=chunk, halo=halo)


def _ssd_chunk_one(xbc_ref, z_ref, dt_ref, cw_ref, cb_ref, dtb_ref, alog_ref, dskip_ref, gn_ref, y_ref, st_ref, xp_sc,
                   *, chunk, halo):
    xp_sc[halo:halo + chunk, :] = xbc_ref[...]
    conv = cb_ref[...]
    for kk in range(CONV_K):
        off = halo - (CONV_K - 1) + kk
        conv = conv + xp_sc[off:off + chunk, :] * cw_ref[kk:kk + 1, :]
    xp_sc[0:halo, :] = xbc_ref[chunk - halo:chunk, :]
    xbc = _silu(conv)
    xs = xbc[:, :SSM_INNER]
    dt = _softplus(dt_ref[...] + dtb_ref[...])
    da = dt * (-jnp.exp(alog_ref[...]))
    ri = lax.broadcasted_iota(jnp.int32, (chunk, chunk), 0)
    ci = lax.broadcasted_iota(jnp.int32, (chunk, chunk), 1)
    tril = ci <= ri
    cs = jnp.dot(jnp.where(tril, 1.0, 0.0), da, precision=lax.Precision.HIGHEST, preferred_element_type=F32)
    cs_t = cs.T
    dt_t = dt.T
    lane = lax.broadcasted_iota(jnp.int32, (1, LANE), 1)
    rowi = lax.broadcasted_iota(jnp.int32, (LANE, 1), 0)
    lo = lane < SSM_HEADDIM
    heads_per_group = SSM_HEADS // SSM_GROUPS
    ys = []
    for pr in range(SSM_HEADS // 2):
        g = (2 * pr) // heads_per_group
        bm = xbc[:, SSM_INNER + g * D_STATE:SSM_INNER + (g + 1) * D_STATE].astype(BF16)
        cm = xbc[:, SSM_INNER + (SSM_GROUPS + g) * D_STATE:SSM_INNER + (SSM_GROUPS + g + 1) * D_STATE].astype(BF16)
        cb = _dot_nt(cm, bm)
        xp = xs[:, pr * LANE:(pr + 1) * LANE]
        ms, xm, ecols, wcols, dlast, dsk = [], [], [], [], [], []
        for hh in range(2):
            hd = 2 * pr + hh
            col = cs[:, hd:hd + 1]
            rowv = cs_t[hd:hd + 1, :]
            seg = jnp.where(tril, col - rowv, -jnp.inf)
            ms.append((cb * jnp.exp(seg) * dt_t[hd:hd + 1, :]).astype(BF16))
            xm.append(jnp.where(lo if hh == 0 else ~lo, xp, 0.0).astype(BF16))
            last = cs[chunk - 1:chunk, hd:hd + 1]
            ecols.append(jnp.exp(col))
            wcols.append(jnp.exp(last - col) * dt[:, hd:hd + 1])
            dlast.append(jnp.exp(last))
            dsk.append(dskip_ref[:, hd:hd + 1])
        y_diag = _dot(jnp.concatenate(ms, axis=1), jnp.concatenate(xm, axis=0))
        st = st_ref[pr * LANE:(pr + 1) * LANE, :]
        y_off = _dot_nt(cm, st.astype(BF16)) * jnp.where(lo, ecols[0], ecols[1])
        xw = (xp * jnp.where(lo, wcols[0], wcols[1])).astype(BF16)
        st_new = _dot_tn(xw, bm)
        decay = jnp.where(rowi < SSM_HEADDIM, dlast[0], dlast[1])
        st_ref[pr * LANE:(pr + 1) * LANE, :] = decay * st + st_new
        ys.append(y_diag + y_off + jnp.where(lo, dsk[0], dsk[1]) * xp)
    y = jnp.concatenate(ys, axis=1)
    y_ref[...] = _gated_norm(y, z_ref[...], gn_ref[...])


def _ssd_chunk(xbc, z, dt, w, *, batch, seq):
    chunk = SSD_CHUNK if seq % SSD_CHUNK == 0 else seq
    nc = seq // chunk
    t = batch * seq
    eb = 2 if batch % 2 == 0 else 1
    seq_blk = lambda width: pl.BlockSpec((eb, chunk, width), lambda b, c: (b, c, 0))
    y, st = pl.pallas_call(
        functools.partial(_ssd_chunk_kernel, chunk=chunk, eb=eb),
        grid=(batch // eb, nc),
        in_specs=[seq_blk(CONV_CH), seq_blk(SSM_INNER), seq_blk(LANE), _full((CONV_K, CONV_CH)), _full((1, CONV_CH)),
                  _full((1, LANE)), _full((1, LANE)), _full((1, LANE)), _full((1, SSM_INNER))],
        out_specs=[seq_blk(SSM_INNER), pl.BlockSpec((eb, SSM_INNER, D_STATE), lambda b, c: (b, 0, 0))],
        out_shape=[jax.ShapeDtypeStruct((batch, seq, SSM_INNER), F32),
                   jax.ShapeDtypeStruct((batch, SSM_INNER, D_STATE), F32)],
        scratch_shapes=[pltpu.VMEM((eb, 8 + chunk, CONV_CH), F32)],
        compiler_params=_cparams(("parallel", "arbitrary")), name="ssd_chunk",
    )(xbc.reshape(batch, seq, CONV_CH), z.reshape(batch, seq, SSM_INNER), dt.reshape(batch, seq, LANE),
      w['conv_w'], w['conv_b'], w['dt_bias_p'], w['a_log_p'], w['d_skip_p'], w['norm_ssm_out'])
    return y.reshape(t, SSM_INNER), st


def _ssd_step_kernel(xbc_ref, cprev_ref, z_ref, dt_ref, st_ref, cw_ref, cb_ref, dtb_ref, alog_ref, dskip_ref, gn_ref,
                     y_ref, cnew_ref, stn_ref, *, nb):
    xraw = xbc_ref[...]
    conv = cb_ref[...] + xraw * cw_ref[CONV_K - 1:CONV_K, :]
    for kk in range(CONV_K - 1):
        conv = conv + cprev_ref[kk] * cw_ref[kk:kk + 1, :]
    for kk in range(CONV_K - 2):
        cnew_ref[kk] = cprev_ref[kk + 1]
    cnew_ref[CONV_K - 2] = xraw
    xbc = _silu(conv)
    xs = xbc[:, :SSM_INNER]
    dt = _softplus(dt_ref[...] + dtb_ref[...])
    decay = jnp.exp(dt * (-jnp.exp(alog_ref[...])))
    tok_l = lax.broadcasted_iota(jnp.int32, (1, nb), 1)
    heads_per_group = SSM_HEADS // SSM_GROUPS
    xs_t = xs.T
    ys = []
    for hd in range(SSM_HEADS):
        g = hd // heads_per_group
        bm = xbc[:, SSM_INNER + g * D_STATE:SSM_INNER + (g + 1) * D_STATE]
        cm = xbc[:, SSM_INNER + (SSM_GROUPS + g) * D_STATE:SSM_INNER + (SSM_GROUPS + g + 1) * D_STATE]
        x_t = xs_t[hd * SSM_HEADDIM:(hd + 1) * SSM_HEADDIM, :]
        dtb = bm * dt[:, hd:hd + 1]
        y_t = jnp.zeros((SSM_HEADDIM, nb), F32)
        for i in range(nb):
            new = decay[i:i + 1, hd:hd + 1] * st_ref[i, hd] + x_t[:, i:i + 1] * dtb[i:i + 1, :]
            stn_ref[i, hd] = new
            yi = lax.dot_general(new, cm, (((1,), (1,)), ((), ())), precision=lax.Precision.HIGHEST,
                                 preferred_element_type=F32)
            y_t = y_t + jnp.where(tok_l == i, yi, 0.0)
        ys.append(y_t)
    y = jnp.concatenate(ys, axis=0).T
    dsk = jnp.concatenate([jnp.broadcast_to(dskip_ref[:, hd:hd + 1], (1, SSM_HEADDIM)) for hd in range(SSM_HEADS)],
                          axis=1)
    y = y + dsk * xs
    y_ref[...] = _gated_norm(y, z_ref[...], gn_ref[...])


def _ssd_step(xbc, conv_prev, z, dt, state, w, *, nb):
    t = xbc.shape[0]
    row = lambda i: (i, 0)
    return pl.pallas_call(
        functools.partial(_ssd_step_kernel, nb=nb),
        grid=(t // nb,),
        in_specs=[pl.BlockSpec((nb, CONV_CH), row), pl.BlockSpec((CONV_K - 1, nb, CONV_CH), lambda i: (0, i, 0)),
                  pl.BlockSpec((nb, SSM_INNER), row), pl.BlockSpec((nb, LANE), row),
                  pl.BlockSpec((nb, SSM_HEADS, SSM_HEADDIM, D_STATE), lambda i: (i, 0, 0, 0)),
                  _full((CONV_K, CONV_CH)), _full((1, CONV_CH)), _full((1, LANE)), _full((1, LANE)), _full((1, LANE)),
                  _full((1, SSM_INNER))],
        out_specs=[pl.BlockSpec((nb, SSM_INNER), row), pl.BlockSpec((CONV_K - 1, nb, CONV_CH), lambda i: (0, i, 0)),
                   pl.BlockSpec((nb, SSM_HEADS, SSM_HEADDIM, D_STATE), lambda i: (i, 0, 0, 0))],
        out_shape=[jax.ShapeDtypeStruct((t, SSM_INNER), F32), jax.ShapeDtypeStruct((CONV_K - 1, t, CONV_CH), F32),
                   jax.ShapeDtypeStruct((t, SSM_HEADS, SSM_HEADDIM, D_STATE), F32)],
        compiler_params=_cparams(("parallel",)), name="ssd_step",
    )(xbc, conv_prev, z, dt, state, w['conv_w'], w['conv_b'], w['dt_bias_p'], w['a_log_p'], w['d_skip_p'],
      w['norm_ssm_out'])


def _out_proj_kernel(*refs, absorbed):
    refs = refs[:-4] + refs[-3:]
    if absorbed:
        (olat_ref, wuv_ref, yn_ref, x_ref, ga_ref, wout_ref, gffn_ref, wr_ref, br_ref, x1_ref, h_ref, route_ref) = refs
        o_attn = _mm(olat_ref[...], wuv_ref[...])
    else:
        (oat_ref, yn_ref, x_ref, ga_ref, wout_ref, gffn_ref, wr_ref, br_ref, x1_ref, h_ref, route_ref) = refs
        o_attn = oat_ref[...]
    n_attn = MLA_HEADS * V_DIM
    d_model = x_ref.shape[1]
    oa = _rms(o_attn, ga_ref[...])
    mix = _mm(oa, wout_ref[0:n_attn, :]) + _mm(yn_ref[...], wout_ref[n_attn:, :])
    x1 = x_ref[...] + mix
    x1_ref[...] = x1
    h = _rms(x1, gffn_ref[...])
    h_ref[:, 0, 0:d_model] = h
    if wr_ref.dtype == BF16:
        h_hi = h.astype(BF16)
        h_lo = (h - h_hi.astype(F32)).astype(BF16)
        logits = _dot(h_hi, wr_ref[0]) + (_dot(h_lo, wr_ref[0]) + _dot(h_hi, wr_ref[1])) + br_ref[...]
    else:
        logits = jnp.dot(h, wr_ref[0], precision=lax.Precision.HIGHEST, preferred_element_type=F32) + br_ref[...]
    lane = lax.broadcasted_iota(jnp.int32, logits.shape, 1)
    neg = -jnp.inf
    gl = jnp.where(lane < N_GROUPS, logits, neg)
    gmax = jnp.max(gl, axis=-1, keepdims=True)
    g_w = 1.0 / jnp.sum(jnp.exp(gl - gmax), axis=-1, keepdims=True)
    g_idx = jnp.min(jnp.where(gl == gmax, lane, LANE), axis=-1, keepdims=True)
    e_lo = N_GROUPS + EXPERTS_PER_GROUP * g_idx
    el = jnp.where((lane >= e_lo) & (lane < e_lo + EXPERTS_PER_GROUP), logits, neg)
    e1 = jnp.max(el, axis=-1, keepdims=True)
    i1 = jnp.min(jnp.where(el == e1, lane, LANE), axis=-1, keepdims=True)
    el2 = jnp.where(lane == i1, neg, el)
    e2 = jnp.max(el2, axis=-1, keepdims=True)
    i2 = jnp.min(jnp.where(el2 == e2, lane, LANE), axis=-1, keepdims=True)
    r = jnp.exp(e2 - e1)
    c1 = g_w / (1.0 + r)
    c2 = g_w * r / (1.0 + r)
    first = i1 < i2
    ia = jnp.where(first, i1, i2)
    ib = jnp.where(first, i2, i1)
    ca = jnp.where(first, c1, c2)
    cb = jnp.where(first, c2, c1)
    la = ia - e_lo
    lb = ib - e_lo
    pair = jnp.where(la == 0, 0, jnp.where(la == 1, 3, 5)) + (lb - la - 1)
    bucket = g_idx * PAIRS_PER_GROUP + pair
    route = jnp.where(lane == R_EA, (ia - N_GROUPS).astype(F32), 0.0)
    route = jnp.where(lane == R_EB, (ib - N_GROUPS).astype(F32), route)
    route = jnp.where(lane == R_CA, ca, route)
    route = jnp.where(lane == R_CB, cb, route)
    route = jnp.where(lane == R_BUCKET, bucket.astype(F32), route)
    route_ref[...] = route
    h_ref[:, 0, d_model:d_model + LANE] = route


def _out_proj(o_in, yn, x, w, *, absorbed, tm, h_buf, h_row0):
    t, d = x.shape
    row = lambda i: (i, 0)
    assert h_row0 % tm == 0
    h_blk0 = h_row0 // tm
    n_mix = MLA_HEADS * V_DIM + SSM_INNER
    if absorbed:
        first = [pl.BlockSpec((tm, MLA_HEADS * KV_LORA), row), _full((MLA_HEADS * KV_LORA, MLA_HEADS * V_DIM))]
        args = [o_in, w['w_uv_bd32']]
    else:
        first = [pl.BlockSpec((tm, MLA_HEADS * V_DIM), row)]
        args = [o_in]
    w_out = w['w_out32'] if absorbed else w['w_out']
    in_specs = first + [pl.BlockSpec((tm, SSM_INNER), row), pl.BlockSpec((tm, d), row), _full((1, MLA_HEADS * V_DIM)),
                        _full((n_mix, d)), _full((1, d)), _full((2, d, LANE)), _full((1, LANE))]
    args += [yn, x, w['norm_attn_out'], w_out, w['norm_ffn'], w['w_router32' if absorbed else 'w_router'],
             w['b_router']]
    in_specs.append(pl.BlockSpec(memory_space=pl.ANY))
    aliases = {len(args): 1}
    args.append(h_buf)
    return pl.pallas_call(
        functools.partial(_out_proj_kernel, absorbed=absorbed),
        grid=(t // tm,), in_specs=in_specs,
        out_specs=[pl.BlockSpec((tm, d), row), pl.BlockSpec((tm, 1, d + LANE), lambda i: (h_blk0 + i, 0, 0)),
                   pl.BlockSpec((tm, LANE), row)],
        out_shape=[jax.ShapeDtypeStruct((t, d), F32), jax.ShapeDtypeStruct(h_buf.shape, F32),
                   jax.ShapeDtypeStruct((t, LANE), F32)],
        input_output_aliases=aliases,
        compiler_params=_cparams(("parallel",)), name="out_proj_abs" if absorbed else "out_proj",
    )(*args)


def _row_copy(src_hbm, src_row, dst, r, sem):
    return pltpu.make_async_copy(src_hbm.at[src_row], dst.at[pl.ds(r, 1), :], sem)


ROW_GROUP = 32


def _row_groups(n_rows, n_valid, body):
    for r0 in range(0, n_rows, ROW_GROUP):
        def group(r0=r0):
            for r in range(r0, min(r0 + ROW_GROUP, n_rows)):
                body(r)
        if n_valid is None:
            group()
        else:
            pl.when(r0 < n_valid)(group)


def _row_gather(idx_ref, src_hbm, dst, sem, n_rows, n_valid):
    _row_groups(n_rows, n_valid, lambda r: _row_copy(src_hbm, idx_ref[0, 0, r], dst, r, sem).start())


def _row_gather_wait(src_hbm, dst, sem, n_rows, n_valid):
    _row_groups(n_rows, n_valid, lambda r: _row_copy(src_hbm, 0, dst, r, sem).wait())


def _gather_begin(i, idx_ref, src_hbm, buf, sem, n_rows, n_valid=None):
    slot = i % 2

    @pl.when(i == 0)
    def _():
        _row_gather(idx_ref, src_hbm, buf.at[0], sem.at[0], n_rows, n_valid)

    _row_gather_wait(src_hbm, buf.at[slot], sem.at[slot], n_rows, n_valid)
    return slot


def _gather_next(slot, idxn_ref, src_hbm, buf, sem, n_rows, n_valid=None):
    _row_gather(idxn_ref, src_hbm, buf.at[1 - slot], sem.at[1 - slot], n_rows, n_valid)


def _gather_drain(i, n, src_hbm, buf, sem, n_rows, n_valid=None):
    @pl.when(i == n - 1)
    def _():
        other = 1 - i % 2
        _row_gather_wait(src_hbm, buf.at[other], sem.at[other], n_rows, n_valid)


def _moe_kernel(ta_ref, tb_ref, tv_ref, tok_ref, tokn_ref, h_hbm, wga_ref, wua_ref, wda_ref, wgb_ref, wub_ref,
                wdb_ref, y_ref, xbuf, sem, *, tm):
    del ta_ref, tb_ref
    i = pl.program_id(0)
    n = pl.num_programs(0)
    n_valid = tv_ref[i]
    n_valid_next = tv_ref[jnp.minimum(i + 1, n - 1)]

    @pl.when(i == 0)
    def _():
        xbuf[...] = jnp.zeros(xbuf.shape, F32)

    slot = _gather_begin(i, tok_ref, h_hbm, xbuf, sem, tm, n_valid)
    _gather_next(slot, tokn_ref, h_hbm, xbuf, sem, tm, n_valid_next)
    d = y_ref.shape[2]

    @pl.when(n_valid > 0)
    def _():
        x = xbuf[slot, :, 0:d].astype(BF16)
        rec = xbuf[slot, :, d:d + LANE]
        y = None
        for (wg_ref, wu_ref, wd_ref), lane_c in (((wga_ref, wua_ref, wda_ref), R_CA),
                                                 ((wgb_ref, wub_ref, wdb_ref), R_CB)):
            act = _silu(_dot(x, wg_ref[0])) * _dot(x, wu_ref[0])
            term = rec[:, lane_c:lane_c + 1] * _dot(act.astype(BF16), wd_ref[0])
            y = term if y is None else y + term
        y_ref[:, 0, :] = y

    @pl.when(n_valid == 0)
    def _():
        y_ref[...] = jnp.zeros(y_ref.shape, F32)

    _gather_drain(i, n, h_hbm, xbuf, sem, tm, n_valid_next)


def _moe(h, tile_ea, tile_eb, tile_valid, row_token, w, *, tm):
    t, _, dp = h.shape
    d = dp - LANE
    n_tiles = tile_ea.shape[0]
    ea = lambda i, ta, tb, tv: (ta[i], 0, 0)
    eb = lambda i, ta, tb, tv: (tb[i], 0, 0)
    grid_spec = pltpu.PrefetchScalarGridSpec(
        num_scalar_prefetch=3, grid=(n_tiles,),
        in_specs=[pl.BlockSpec((1, 1, tm), lambda i, ta, tb, tv: (i, 0, 0), memory_space=pltpu.SMEM),
                  pl.BlockSpec((1, 1, tm), lambda i, ta, tb, tv: (jnp.minimum(i + 1, n_tiles - 1), 0, 0),
                               memory_space=pltpu.SMEM),
                  pl.BlockSpec(memory_space=pl.ANY),
                  pl.BlockSpec((1, d, EXPERT_FF), ea), pl.BlockSpec((1, d, EXPERT_FF), ea),
                  pl.BlockSpec((1, EXPERT_FF, d), ea),
                  pl.BlockSpec((1, d, EXPERT_FF), eb), pl.BlockSpec((1, d, EXPERT_FF), eb),
                  pl.BlockSpec((1, EXPERT_FF, d), eb)],
        out_specs=pl.BlockSpec((tm, 1, d), lambda i, ta, tb, tv: (i, 0, 0)),
        scratch_shapes=[pltpu.VMEM((2, tm, dp), F32), pltpu.SemaphoreType.DMA((2,))])
    rt = row_token.reshape(n_tiles, 1, tm)
    wts = (w['w_exp_gate'], w['w_exp_up'], w['w_exp_down'])
    return pl.pallas_call(
        functools.partial(_moe_kernel, tm=tm), grid_spec=grid_spec,
        out_shape=jax.ShapeDtypeStruct((n_tiles * tm, 1, d), F32),
        compiler_params=_cparams(("arbitrary",)), name="moe",
    )(tile_ea, tile_eb, tile_valid, rt, rt, h, *wts, *wts)


def _moe_plan(route, *, tm):
    t = route.shape[0]
    bids = route[:, R_BUCKET].astype(jnp.int32)
    onehot = (bids[:, None] == jnp.arange(N_BUCKETS, dtype=jnp.int32)[None, :]).astype(F32)
    blk = _tile(t, LANE)
    oh = onehot.reshape(-1, blk, N_BUCKETS)
    intra = jnp.einsum('ts,bse->bte', jnp.tril(jnp.ones((blk, blk), F32)), oh)
    tot = intra[:, -1, :]
    csum = (intra + (jnp.cumsum(tot, axis=0) - tot)[:, None, :]).reshape(-1, N_BUCKETS)
    rank = jnp.sum(onehot * (csum - 1.0), axis=1).astype(jnp.int32)
    counts = csum[-1].astype(jnp.int32)
    padded = ((counts + tm - 1) // tm) * tm
    ends = jnp.cumsum(padded)
    offs = ends - padded
    pos = offs[bids] + rank
    n_tiles = -(-t // tm) + N_BUCKETS
    row_token = jnp.zeros((n_tiles * tm,), jnp.int32).at[pos].set(jnp.arange(t, dtype=jnp.int32))
    tile_start = jnp.arange(n_tiles, dtype=jnp.int32) * tm
    tile_bucket = jnp.sum((ends[None, :] <= tile_start[:, None]).astype(jnp.int32), axis=1)
    tile_bucket = jnp.minimum(tile_bucket, N_BUCKETS - 1)
    group0 = (tile_bucket // PAIRS_PER_GROUP) * EXPERTS_PER_GROUP
    pair = tile_bucket % PAIRS_PER_GROUP
    tile_ea = group0 + jnp.asarray(PAIR_A, jnp.int32)[pair]
    tile_eb = group0 + jnp.asarray(PAIR_B, jnp.int32)[pair]
    tile_valid = jnp.clip((offs + counts)[tile_bucket] - tile_start, 0, tm).astype(jnp.int32)
    return tile_ea, tile_eb, tile_valid, row_token, pos


def _final_kernel(pos_ref, posn_ref, x1_ref, p_ref, y_hbm, gple_ref, wgate_ref, wproj_ref, gfin_ref,
                  out_ref, ybuf, sem, *, tm):
    i = pl.program_id(0)
    n = pl.num_programs(0)
    slot = _gather_begin(i, pos_ref, y_hbm, ybuf, sem, tm)
    _gather_next(slot, posn_ref, y_hbm, ybuf, sem, tm)
    x2 = x1_ref[...] + ybuf[slot]
    gate_in = _rms(x2, gple_ref[...]).astype(BF16)
    gate = 1.0 / (1.0 + jnp.exp(-_dot(gate_in, wgate_ref[...])))
    x3 = x2 + gate * _dot(p_ref[...].astype(BF16), wproj_ref[...])
    out_ref[...] = _rms(x3, gfin_ref[...])
    _gather_drain(i, n, y_hbm, ybuf, sem, tm)


def _final(x1, p, y_sorted, pos, w, *, tm):
    t, d = x1.shape
    nt = t // tm
    ple = p.shape[1]
    pos_t = pos.reshape(nt, 1, tm)
    row = lambda i: (i, 0)
    in_specs = [pl.BlockSpec((1, 1, tm), lambda i: (i, 0, 0), memory_space=pltpu.SMEM),
                pl.BlockSpec((1, 1, tm), lambda i: (jnp.minimum(i + 1, nt - 1), 0, 0), memory_space=pltpu.SMEM),
                pl.BlockSpec((tm, d), row), pl.BlockSpec((tm, ple), row),
                pl.BlockSpec(memory_space=pl.ANY), _full((1, d)), _full((d, d)), _full((ple, d)), _full((1, d))]
    return pl.pallas_call(
        functools.partial(_final_kernel, tm=tm), grid=(nt,), in_specs=in_specs,
        out_specs=pl.BlockSpec((tm, d), row), out_shape=jax.ShapeDtypeStruct((t, d), F32),
        scratch_shapes=[pltpu.VMEM((2, tm, d), F32), pltpu.SemaphoreType.DMA((2,))],
        compiler_params=_cparams(("arbitrary",)), name="final",
    )(pos_t, pos_t, x1, p, y_sorted, w['norm_ple'], w['w_ple_gate'], w['w_ple_proj'], w['norm_final'])


def _pad_lanes(v, width=LANE):
    v = v.reshape(1, -1)
    return jnp.pad(v, ((0, 0), (0, width - v.shape[1])))


def _rot_cols(wr):
    half = QK_ROPE // 2
    return jnp.concatenate([-wr[..., half:], wr[..., :half]], axis=-1)


def _prep_weights(norm_mix, w_in, norm_q, w_uq, norm_kv, w_ukv, norm_attn_out, conv_w, conv_b, dt_bias, a_log, d_skip,
                  norm_ssm_out, w_out, norm_ffn, w_group_router, b_group_router, w_expert_router, b_expert_router,
                  w_exp_gate, w_exp_up, w_exp_down, norm_ple, w_ple_gate, w_ple_proj, norm_final):
    d = w_in.shape[0]
    splits = [Q_LORA, KV_LORA, QK_ROPE, SSM_INNER, CONV_CH, SSM_HEADS]
    bounds = [0]
    for s in splits:
        bounds.append(bounds[-1] + s)
    w_cq, w_ckv, w_kr, w_z, w_xbc, w_dt = [w_in[:, bounds[j]:bounds[j + 1]] for j in range(6)]
    zpad = lambda n: jnp.zeros((d, n), F32)
    place = lambda m: jnp.concatenate([zpad(ROPE_LANE0), m, zpad(LANE - ROPE_LANE0 - QK_ROPE)], axis=1)
    w_in_p = jnp.concatenate([w_cq, w_ckv, place(w_kr), place(_rot_cols(w_kr)), w_z, w_xbc, w_dt,
                              zpad(LANE - SSM_HEADS)], axis=1)
    nope, rp = w_uq[..., :QK_NOPE], w_uq[..., QK_NOPE:]
    zq = lambda n: jnp.zeros((Q_LORA, MLA_HEADS, n), F32)
    tail = HEAD_PAD - QK_NOPE - QK_ROPE
    wq_a = jnp.concatenate([nope, rp, zq(tail)], axis=-1).reshape(Q_LORA, -1)
    wq_b = jnp.concatenate([zq(QK_NOPE), _rot_cols(rp), zq(tail)], axis=-1).reshape(Q_LORA, -1)
    w_uk, w_uv = w_ukv[..., :QK_NOPE], w_ukv[..., QK_NOPE:]
    wk_p = jnp.concatenate([w_uk, jnp.zeros((KV_LORA, MLA_HEADS, HEAD_PAD - QK_NOPE), F32)], axis=-1)
    w_kv_p = jnp.concatenate([wk_p.reshape(KV_LORA, -1), w_uv.reshape(KV_LORA, -1)], axis=1).astype(BF16)
    w_abs = jnp.zeros((MLA_HEADS, HEAD_PAD, Q_ABS), F32)
    w_abs = w_abs.at[:, :QK_NOPE, :KV_LORA].set(jnp.transpose(w_uk, (1, 2, 0)))
    w_abs = w_abs.at[:, QK_NOPE:QK_NOPE + QK_ROPE, KV_LORA:KV_LORA + QK_ROPE].set(
        jnp.broadcast_to(jnp.eye(QK_ROPE, dtype=F32), (MLA_HEADS, QK_ROPE, QK_ROPE)))
    w_uv_bd = jnp.zeros((MLA_HEADS, KV_LORA, MLA_HEADS, V_DIM), F32)
    hidx = jnp.arange(MLA_HEADS)
    w_uv_bd = w_uv_bd.at[hidx, :, hidx, :].set(jnp.transpose(w_uv, (1, 0, 2)))
    w_router = jnp.concatenate([w_group_router, w_expert_router,
                                jnp.zeros((d, LANE - N_GROUPS - N_EXPERTS), F32)], axis=1)
    b_router = _pad_lanes(jnp.concatenate([b_group_router, b_expert_router]))
    w_router_hi = w_router.astype(BF16)
    w_router_lo = (w_router - w_router_hi.astype(F32)).astype(BF16)
    w_router32 = jnp.stack([w_router, jnp.zeros_like(w_router)])
    w_router = jnp.stack([w_router_hi, w_router_lo])
    return dict(
        norm_mix=norm_mix.reshape(1, -1), w_in_p=w_in_p.astype(BF16), w_in_p32=w_in_p, norm_q=norm_q.reshape(1, -1),
        wq_a=wq_a.astype(BF16), wq_b=wq_b.astype(BF16), wq_a32=wq_a, wq_b32=wq_b,
        norm_kv=norm_kv.reshape(1, -1), w_kv_p=w_kv_p, w_abs32=w_abs,
        w_uv_bd32=w_uv_bd.reshape(MLA_HEADS * KV_LORA, MLA_HEADS * V_DIM),
        norm_attn_out=norm_attn_out.reshape(1, -1), conv_w=conv_w, conv_b=conv_b.reshape(1, -1),
        dt_bias_p=_pad_lanes(dt_bias), a_log_p=_pad_lanes(a_log), d_skip_p=_pad_lanes(d_skip),
        norm_ssm_out=norm_ssm_out.reshape(1, -1), w_out=w_out.astype(BF16), w_out32=w_out,
        norm_ffn=norm_ffn.reshape(1, -1),
        w_router=w_router, w_router32=w_router32, b_router=b_router, w_exp_gate=w_exp_gate.astype(BF16), w_exp_up=w_exp_up.astype(BF16),
        w_exp_down=w_exp_down.astype(BF16), norm_ple=norm_ple.reshape(1, -1), w_ple_gate=w_ple_gate.astype(BF16),
        w_ple_proj=w_ple_proj.astype(BF16), norm_final=norm_final.reshape(1, -1))


def _rope_tables(pos):
    half = QK_ROPE // 2
    inv = ROPE_THETA ** (-jnp.arange(half, dtype=F32) / half)
    ang = pos.astype(F32)[:, None] * inv[None, :]
    place = lambda m: jnp.pad(jnp.concatenate([m, m], axis=1), ((0, 0), (ROPE_LANE0, LANE - ROPE_LANE0 - QK_ROPE)))
    return place(jnp.cos(ang)), place(jnp.sin(ang))


def _tile(n, pref):
    return pref if n % pref == 0 else n


def kernel(x_prompt, x_sample, p_prompt, p_sample, cache_ckv, cache_krope, state_conv, state_ssm, page_table, norm_mix, w_in, norm_q, w_uq, norm_kv, w_ukv, norm_attn_out, conv_w, conv_b, dt_bias, a_log, d_skip, norm_ssm_out, w_out, norm_ffn, w_group_router, b_group_router, w_expert_router, b_expert_router, w_exp_gate, w_exp_up, w_exp_down, norm_ple, w_ple_gate, w_ple_proj, norm_final):
    depth = w_in.shape[0]
    assert depth == 1, "single-layer stack"
    bp, sp, d = x_prompt.shape
    bs, ls, _ = x_sample.shape
    assert ls == 1, "decode step handles one new token per sequence"
    n_pool, page, _ = cache_ckv.shape[1:]
    n_pages = page_table.shape[1]
    past_len = n_pages * page
    w = _prep_weights(norm_mix[0], w_in[0], norm_q[0], w_uq[0], norm_kv[0], w_ukv[0], norm_attn_out[0], conv_w[0],
                      conv_b[0], dt_bias[0], a_log[0], d_skip[0], norm_ssm_out[0], w_out[0], norm_ffn[0],
                      w_group_router[0], b_group_router[0], w_expert_router[0], b_expert_router[0], w_exp_gate[0],
                      w_exp_up[0], w_exp_down[0], norm_ple[0], w_ple_gate[0], w_ple_proj[0], norm_final)

    tp = bp * sp
    xp = x_prompt.reshape(tp, d)
    tm_p = _tile(sp, 512)
    cos_p, sin_p = _rope_tables(jnp.arange(sp))
    q, k, v, ckv_p, krp_p, z_p, xbc_p, dt_p = _in_proj(xp, cos_p, sin_p, w, absorb=False, tm=tm_p,
                                                       n_pos_blocks=sp // tm_p)
    tq = _tile(sp, 1024)
    o_attn = _flash(q, k, v, batch=bp, seq=sp, tq=tq, tk=tq)
    yn_p, ssm_p = _ssd_chunk(xbc_p, z_p, dt_p, w, batch=bp, seq=sp)
    h_all = jnp.zeros((tp + bs, 1, d + LANE), F32)
    x1_p, h_all, route_p = _out_proj(o_attn, yn_p, xp, w, absorbed=False, tm=_tile(tp, 512), h_buf=h_all, h_row0=0)

    xs = x_sample.reshape(bs, d)
    cos_s, sin_s = _rope_tables(jnp.full((bs,), past_len, jnp.int32))
    q_abs, ckv_s, krp_s, z_s, xbc_s, dt_s = _in_proj(xs, cos_s, sin_s, w, absorb=True, tm=bs, n_pos_blocks=1)
    ch = n_pages if n_pages <= 128 else (64 if n_pages % 64 == 0 else n_pages)
    o_lat = _paged(q_abs.reshape(bs, MLA_HEADS, Q_ABS), ckv_s.reshape(bs, 1, KV_LORA), krp_s.reshape(bs, 1, LANE),
                   cache_ckv.reshape(n_pool, page, KV_LORA), jnp.swapaxes(cache_krope, 2, 3).reshape(n_pool, QK_ROPE, page), page_table,
                   ch=ch, nslot=2, eb=1)
    conv_prev = jnp.transpose(state_conv[0], (1, 0, 2))
    yn_s, conv_new, ssm_s = _ssd_step(xbc_s, conv_prev, z_s, dt_s, state_ssm[0], w, nb=_tile(bs, 8))
    x1_s, h_all, route_s = _out_proj(o_lat.reshape(bs, MLA_HEADS * KV_LORA), yn_s, xs, w, absorbed=True, tm=bs,
                                     h_buf=h_all, h_row0=tp)

    tm_moe = 256
    tile_ea, tile_eb, tile_valid, row_token, pos = _moe_plan(jnp.concatenate([route_p, route_s], axis=0), tm=tm_moe)
    y_sorted = _moe(h_all, tile_ea, tile_eb, tile_valid, row_token, w, tm=tm_moe)
    y_prompt = _final(x1_p, p_prompt[0].reshape(tp, -1), y_sorted, pos[:tp], w, tm=_tile(tp, 512))
    y_sample = _final(x1_s, p_sample[0].reshape(bs, -1), y_sorted, pos[tp:], w, tm=bs)

    rope_sl = slice(ROPE_LANE0, ROPE_LANE0 + QK_ROPE)
    return (y_prompt.reshape(bp, sp, d), y_sample.reshape(bs, ls, d),
            ckv_p.reshape(1, bp, sp, KV_LORA), krp_p[:, rope_sl].reshape(1, bp, sp, QK_ROPE),
            xbc_p.reshape(bp, sp, CONV_CH)[:, sp - (CONV_K - 1):, :][None],
            ssm_p.reshape(1, bp, SSM_HEADS, SSM_HEADDIM, D_STATE),
            ckv_s.reshape(1, bs, ls, KV_LORA), krp_s[:, rope_sl].reshape(1, bs, ls, QK_ROPE),
            jnp.transpose(conv_new, (1, 0, 2))[None], ssm_s[None])
```

```python
import functools
import math

import jax
import jax.numpy as jnp
from jax import lax
from jax.experimental import pallas as pl
from jax.experimental.pallas import tpu as pltpu

F32 = jnp.float32
BF16 = jnp.bfloat16

MLA_HEADS = 8
QK_NOPE = 64
QK_ROPE = 32
V_DIM = 64
Q_LORA = 384
KV_LORA = 256
ROPE_THETA = 10000.0
ATTN_SCALE = 1.0 / math.sqrt(QK_NOPE + QK_ROPE)
LOG2E = math.log2(math.e)
Q_SCALE = ATTN_SCALE * LOG2E
SSM_HEADS = 8
SSM_HEADDIM = 64
SSM_INNER = SSM_HEADS * SSM_HEADDIM
SSM_GROUPS = 2
D_STATE = 128
CONV_K = 4
CONV_CH = SSM_INNER + 2 * SSM_GROUPS * D_STATE
SSD_CHUNK = 128
N_GROUPS = 4
EXPERTS_PER_GROUP = 4
N_EXPERTS = N_GROUPS * EXPERTS_PER_GROUP
EXPERT_FF = 512
EPS = 1e-6
PAIR_A = (0, 0, 0, 1, 1, 2)
PAIR_B = (1, 2, 3, 2, 3, 3)
PAIRS_PER_GROUP = len(PAIR_A)
N_BUCKETS = N_GROUPS * PAIRS_PER_GROUP
R_EA, R_EB, R_CA, R_CB, R_BUCKET = 0, 1, 2, 3, 4

LANE = 128
HEAD_PAD = 128
ROPE_LANE0 = QK_NOPE
Q_ABS = KV_LORA + LANE

C_CQ = 0
C_CKV = C_CQ + Q_LORA
C_KRA = C_CKV + KV_LORA
C_KRB = C_KRA + LANE
C_Z = C_KRB + LANE
C_XBC = C_Z + SSM_INNER
C_DT = C_XBC + CONV_CH
IN_COLS_P = C_DT + LANE

VMEM_LIMIT = 56 * 1024 * 1024


def _cparams(sem, vmem=VMEM_LIMIT):
    return pltpu.CompilerParams(dimension_semantics=sem, vmem_limit_bytes=vmem)


def _rms(x, g):
    return x * lax.rsqrt(jnp.mean(x * x, axis=-1, keepdims=True) + EPS) * g


def _dot(a, b):
    return jnp.dot(a, b, preferred_element_type=F32)


def _mm(a, w):
    if w.dtype == BF16:
        return jnp.dot(a.astype(BF16), w, preferred_element_type=F32)
    return jnp.dot(a, w, precision=lax.Precision.HIGHEST, preferred_element_type=F32)


def _dot_nt(a, b):
    return lax.dot_general(a, b, (((1,), (1,)), ((), ())), preferred_element_type=F32)


def _dot_tn(a, b):
    return lax.dot_general(a, b, (((0,), (0,)), ((), ())), preferred_element_type=F32)


def _silu(x):
    return x * (1.0 / (1.0 + jnp.exp(-x)))


def _softplus(x):
    return jnp.maximum(x, 0.0) + jnp.log(1.0 + jnp.exp(-jnp.abs(x)))


def _full(shape):
    nd = len(shape)
    return pl.BlockSpec(shape, lambda *a: (0,) * nd)


def _in_proj_kernel(x_ref, gmix_ref, win_ref, gq_ref, wqa_ref, wqb_ref, gkv_ref, wkv_ref, cos_ref, sin_ref,
                    *refs, absorb):
    if absorb:
        wabs_ref, q_ref, ckv_ref, krp_ref, z_ref, xbc_ref, dt_ref = refs
    else:
        q_ref, k_ref, v_ref, ckv_ref, krp_ref, z_ref, xbc_ref, dt_ref = refs
    h = _rms(x_ref[...], gmix_ref[...])
    u = _mm(h, win_ref[...])
    cq = _rms(u[:, C_CQ:C_CQ + Q_LORA], gq_ref[...])
    qa = _mm(cq, wqa_ref[...])
    qb = _mm(cq, wqb_ref[...])
    cos = cos_ref[...]
    sin = sin_ref[...]
    lane = lax.broadcasted_iota(jnp.int32, (1, LANE), 1)
    q_cos = Q_SCALE * (jnp.where(lane < QK_NOPE, 1.0, 0.0) + cos)
    q_sin = Q_SCALE * sin
    ckv = _rms(u[:, C_CKV:C_CKV + KV_LORA], gkv_ref[...])
    ckv_ref[...] = ckv
    krp = u[:, C_KRA:C_KRA + LANE] * cos + u[:, C_KRB:C_KRB + LANE] * sin
    krp_ref[...] = krp
    z_ref[...] = u[:, C_Z:C_Z + SSM_INNER]
    xbc_ref[...] = u[:, C_XBC:C_XBC + CONV_CH]
    dt_ref[...] = u[:, C_DT:C_DT + LANE]
    for hd in range(MLA_HEADS):
        sl = slice(hd * HEAD_PAD, (hd + 1) * HEAD_PAD)
        qh = qa[:, sl] * q_cos + qb[:, sl] * q_sin
        if absorb:
            q_ref[:, hd * Q_ABS:(hd + 1) * Q_ABS] = _mm(qh, wabs_ref[hd]).astype(BF16)
        else:
            q_ref[:, sl] = qh.astype(BF16)
    if not absorb:
        kv = _mm(ckv, wkv_ref[...])
        for hd in range(MLA_HEADS):
            sl = slice(hd * HEAD_PAD, (hd + 1) * HEAD_PAD)
            k_ref[:, sl] = (kv[:, sl] + krp).astype(BF16)
        v_ref[...] = kv[:, MLA_HEADS * HEAD_PAD:].astype(BF16)


def _in_proj(x, cos, sin, w, *, absorb, tm, n_pos_blocks):
    t, d = x.shape
    nt = t // tm
    row = lambda i: (i, 0)
    pos = lambda i: (i % n_pos_blocks, 0)
    in_specs = [
        pl.BlockSpec((tm, d), row), _full((1, d)), _full((d, IN_COLS_P)), _full((1, Q_LORA)),
        _full((Q_LORA, MLA_HEADS * HEAD_PAD)), _full((Q_LORA, MLA_HEADS * HEAD_PAD)), _full((1, KV_LORA)),
        _full((KV_LORA, MLA_HEADS * (HEAD_PAD + V_DIM))), pl.BlockSpec((tm, LANE), pos), pl.BlockSpec((tm, LANE), pos),
    ]
    sfx = '32' if absorb else ''
    args = [x, w['norm_mix'], w['w_in_p' + sfx], w['norm_q'], w['wq_a' + sfx], w['wq_b' + sfx], w['norm_kv'],
            w['w_kv_p'], cos, sin]
    tail_shapes = [
        jax.ShapeDtypeStruct((t, KV_LORA), F32), jax.ShapeDtypeStruct((t, LANE), F32),
        jax.ShapeDtypeStruct((t, SSM_INNER), F32), jax.ShapeDtypeStruct((t, CONV_CH), F32),
        jax.ShapeDtypeStruct((t, LANE), F32),
    ]
    tail_specs = [pl.BlockSpec((tm, KV_LORA), row), pl.BlockSpec((tm, LANE), row), pl.BlockSpec((tm, SSM_INNER), row),
                  pl.BlockSpec((tm, CONV_CH), row), pl.BlockSpec((tm, LANE), row)]
    if absorb:
        in_specs.append(_full((MLA_HEADS, HEAD_PAD, Q_ABS)))
        args.append(w['w_abs32'])
        out_shape = [jax.ShapeDtypeStruct((t, MLA_HEADS * Q_ABS), BF16)] + tail_shapes
        out_specs = [pl.BlockSpec((tm, MLA_HEADS * Q_ABS), row)] + tail_specs
    else:
        out_shape = [jax.ShapeDtypeStruct((t, MLA_HEADS * HEAD_PAD), BF16),
                     jax.ShapeDtypeStruct((t, MLA_HEADS * HEAD_PAD), BF16),
                     jax.ShapeDtypeStruct((t, MLA_HEADS * V_DIM), BF16)] + tail_shapes
        out_specs = [pl.BlockSpec((tm, MLA_HEADS * HEAD_PAD), row), pl.BlockSpec((tm, MLA_HEADS * HEAD_PAD), row),
                     pl.BlockSpec((tm, MLA_HEADS * V_DIM), row)] + tail_specs
    return pl.pallas_call(
        functools.partial(_in_proj_kernel, absorb=absorb),
        grid=(nt,), in_specs=in_specs, out_specs=out_specs, out_shape=out_shape,
        compiler_params=_cparams(("parallel",)), name="in_proj_abs" if absorb else "in_proj",
    )(*args)


def _flash_kernel(q_ref, k_ref, v_ref, o_ref, *, tq, tk):
    qi = pl.program_id(2)
    lane = lax.broadcasted_iota(jnp.int32, (1, LANE), 1)
    qs = [q_ref[:, hh * HEAD_PAD:(hh + 1) * HEAD_PAD] for hh in range(2)]

    def block(carry, ks, width, r0, masked):
        rows = tq - r0
        v = v_ref[pl.ds(ks, width), :]
        if masked:
            row = qi * tq + r0 + lax.broadcasted_iota(jnp.int32, (rows, width), 0)
            col = ks + lax.broadcasted_iota(jnp.int32, (rows, width), 1)
            keep = col <= row
        out = []
        for hh in range(2):
            m0, l0, acc0 = carry[hh]
            m, l, acc = m0[r0:], l0[r0:], acc0[r0:]
            s = _dot_nt(qs[hh][r0:], k_ref[pl.ds(ks, width), hh * HEAD_PAD:(hh + 1) * HEAD_PAD])
            if masked:
                s = jnp.where(keep, s, -jnp.inf)
            m_new = jnp.maximum(m, jnp.max(s, axis=-1, keepdims=True))
            alpha = jnp.exp2(m - m_new)
            p = jnp.exp2(s - m_new)
            l = alpha * l + jnp.sum(p, axis=-1, keepdims=True)
            acc = alpha * acc + _dot(p.astype(BF16), v)
            if r0:
                m_new, l, acc = (jnp.concatenate([old[:r0], new], axis=0)
                                 for old, new in ((m0, m_new), (l0, l), (acc0, acc)))
            out.append((m_new, l, acc))
        return tuple(out)

    def step(ki, carry, masked):
        return block(carry, pl.multiple_of(ki * tk, tk), tk, 0, masked)

    init1 = (jnp.full((tq, 1), -jnp.inf, F32), jnp.zeros((tq, 1), F32), jnp.zeros((tq, 2 * V_DIM), F32))
    n_full = (qi * tq) // tk
    carry = lax.fori_loop(0, n_full, functools.partial(step, masked=False), (init1, init1))
    half = tq // 2
    if tq == tk and half % 16 == 0:
        d0 = pl.multiple_of(qi * tq, tq)
        carry = block(carry, d0, half, 0, True)
        carry = block(carry, pl.multiple_of(d0 + half, half), half, half, True)
    else:
        for d in range((tq + tk - 1) // tk):
            carry = step(n_full + d, carry, True)
    o0 = carry[0][2] / carry[0][1]
    o1 = carry[1][2] / carry[1][1]
    o_ref[...] = jnp.where(lane < V_DIM, o0, o1)


def _flash(q, k, v, *, batch, seq, tq, tk):
    t = batch * seq
    nq = seq // tq
    npair = MLA_HEADS // 2
    return pl.pallas_call(
        functools.partial(_flash_kernel, tq=tq, tk=tk),
        grid=(batch, npair, nq),
        in_specs=[pl.BlockSpec((tq, 2 * HEAD_PAD), lambda b, hp, i: (b * nq + i, hp)),
                  pl.BlockSpec((seq, 2 * HEAD_PAD), lambda b, hp, i: (b, hp)),
                  pl.BlockSpec((seq, 2 * V_DIM), lambda b, hp, i: (b, hp))],
        out_specs=pl.BlockSpec((tq, 2 * V_DIM), lambda b, hp, i: (b * nq + i, hp)),
        out_shape=jax.ShapeDtypeStruct((t, MLA_HEADS * V_DIM), F32),
        compiler_params=_cparams(("parallel", "parallel", "arbitrary")), name="flash",
    )(q, k, v)


def _paged_kernel(pt_ref, q_ref, ckvn_ref, krn_ref, cc_hbm, ck_hbm, o_ref, cbuf, kbuf, sem,
                  *, n_chunks, ch, page, nslot, n_steps, eb):
    step = pl.program_id(0)
    total = n_steps * n_chunks

    def copies(g, slot):
        gs = g // n_chunks
        gc = g % n_chunks
        out = []
        for e in range(eb):
            for j in range(ch):
                pg = pt_ref[gs * eb + e, gc * ch + j]
                out.append(pltpu.make_async_copy(cc_hbm.at[pg], cbuf.at[slot, e, pl.ds(j * page, page), :],
                                                 sem.at[0, slot]))
                out.append(pltpu.make_async_copy(ck_hbm.at[pg], kbuf.at[slot, e, :, pl.ds(j * page, page)],
                                                 sem.at[1, slot]))
        return out

    def start(g):
        @pl.when(g < total)
        def _():
            for c in copies(g, g % nslot):
                c.start()

    @pl.when(step == 0)
    def _():
        for g0 in range(nslot - 1):
            start(jnp.int32(g0))

    qs = [q_ref[e] for e in range(eb)]

    def chunk(c, carry):
        g = step * n_chunks + c
        start(g + nslot - 1)
        slot = g % nslot
        for cp in copies(g, slot):
            cp.wait()
        new = []
        for e in range(eb):
            m, l, acc = carry[e]
            kc = cbuf[slot, e].astype(BF16)
            kr_t = kbuf[slot, e].astype(BF16)
            s = _dot_nt(qs[e][:, :KV_LORA], kc) + _dot(qs[e][:, KV_LORA:KV_LORA + QK_ROPE], kr_t)
            m_new = jnp.maximum(m, jnp.max(s, axis=-1, keepdims=True))
            alpha = jnp.exp2(m - m_new)
            p = jnp.exp2(s - m_new)
            l = alpha * l + jnp.sum(p, axis=-1, keepdims=True)
            acc = alpha * acc + _dot(p.astype(BF16), kc)
            new.append((m_new, l, acc))
        return tuple(new)

    init1 = (jnp.full((MLA_HEADS, 1), -jnp.inf, F32), jnp.zeros((MLA_HEADS, 1), F32),
             jnp.zeros((MLA_HEADS, KV_LORA), F32))
    carry = lax.fori_loop(0, n_chunks, chunk, (init1,) * eb)

    for e in range(eb):
        m, l, acc = carry[e]
        q_lat = qs[e][:, :KV_LORA].astype(F32)
        q_rope = qs[e][:, KV_LORA:KV_LORA + QK_ROPE].astype(F32)
        ckvn = ckvn_ref[e]
        krn = krn_ref[e][:, ROPE_LANE0:ROPE_LANE0 + QK_ROPE]
        s_new = jnp.sum(q_lat * ckvn, axis=-1, keepdims=True) + jnp.sum(q_rope * krn, axis=-1, keepdims=True)
        m_new = jnp.maximum(m, s_new)
        alpha = jnp.exp2(m - m_new)
        p_new = jnp.exp2(s_new - m_new)
        o_ref[e] = (alpha * acc + p_new * ckvn) / (alpha * l + p_new)


def _paged(q_abs, ckv_new, krp_new, cache_ckv, cache_kr, page_table, *, ch, nslot, eb):
    nb, n_pages = page_table.shape
    page = cache_ckv.shape[1]
    assert cache_kr.shape[1:] == (QK_ROPE, page)
    n_chunks = n_pages // ch
    n_steps = nb // eb
    blk = lambda *tail: pl.BlockSpec((eb,) + tail, lambda s, pt: (s,) + (0,) * len(tail))
    grid_spec = pltpu.PrefetchScalarGridSpec(
        num_scalar_prefetch=1, grid=(n_steps,),
        in_specs=[blk(MLA_HEADS, Q_ABS), blk(1, KV_LORA), blk(1, LANE),
                  pl.BlockSpec(memory_space=pl.ANY), pl.BlockSpec(memory_space=pl.ANY)],
        out_specs=blk(MLA_HEADS, KV_LORA),
        scratch_shapes=[pltpu.VMEM((nslot, eb, ch * page, KV_LORA), F32),
                        pltpu.VMEM((nslot, eb, QK_ROPE, ch * page), F32), pltpu.SemaphoreType.DMA((2, nslot))])
    return pl.pallas_call(
        functools.partial(_paged_kernel, n_chunks=n_chunks, ch=ch, page=page, nslot=nslot, n_steps=n_steps, eb=eb),
        grid_spec=grid_spec, out_shape=jax.ShapeDtypeStruct((nb, MLA_HEADS, KV_LORA), F32),
        compiler_params=_cparams(("arbitrary",)), name="paged_attn",
    )(page_table, q_abs, ckv_new, krp_new, cache_ckv, cache_kr)


def _gated_norm(y, z, g):
    return _rms(y * _silu(z), g)


def _ssd_chunk_kernel(xbc_ref, z_ref, dt_ref, cw_ref, cb_ref, dtb_ref, alog_ref, dskip_ref, gn_ref,
                      y_ref, st_ref, xp_sc, *, chunk, eb):
    c = pl.program_id(1)
    halo = 8

    @pl.when(c == 0)
    def _():
        xp_sc[:, 0:halo, :] = jnp.zeros((eb, halo, CONV_CH), F32)
        st_ref[...] = jnp.zeros(st_ref.shape, F32)

    for e in range(eb):
        _ssd_chunk_one(xbc_ref.at[e], z_ref.at[e], dt_ref.at[e], cw_ref, cb_ref, dtb_ref, alog_ref, dskip_ref, gn_ref,
                       y_ref.at[e], st_ref.at[e], xp_sc.at[e], chunk=chunk, halo=halo)


def _ssd_chunk_one(xbc_ref, z_ref, dt_ref, cw_ref, cb_ref, dtb_ref, alog_ref, dskip_ref, gn_ref, y_ref, st_ref, xp_sc,
                   *, chunk, halo):
    xp_sc[halo:halo + chunk, :] = xbc_ref[...]
    conv = cb_ref[...]
    for kk in range(CONV_K):
        off = halo - (CONV_K - 1) + kk
        conv = conv + xp_sc[off:off + chunk, :] * cw_ref[kk:kk + 1, :]
    xp_sc[0:halo, :] = xbc_ref[chunk - halo:chunk, :]
    xbc = _silu(conv)
    xs = xbc[:, :SSM_INNER]
    dt = _softplus(dt_ref[...] + dtb_ref[...])
    da = dt * (-jnp.exp(alog_ref[...]))
    ri = lax.broadcasted_iota(jnp.int32, (chunk, chunk), 0)
    ci = lax.broadcasted_iota(jnp.int32, (chunk, chunk), 1)
    tril = ci <= ri
    cs = jnp.dot(jnp.where(tril, 1.0, 0.0), da, precision=lax.Precision.HIGHEST, preferred_element_type=F32)
    cs_t = cs.T
    dt_t = dt.T
    lane = lax.broadcasted_iota(jnp.int32, (1, LANE), 1)
    rowi = lax.broadcasted_iota(jnp.int32, (LANE, 1), 0)
    lo = lane < SSM_HEADDIM
    heads_per_group = SSM_HEADS // SSM_GROUPS
    ys = []
    for pr in range(SSM_HEADS // 2):
        g = (2 * pr) // heads_per_group
        bm = xbc[:, SSM_INNER + g * D_STATE:SSM_INNER + (g + 1) * D_STATE].astype(BF16)
        cm = xbc[:, SSM_INNER + (SSM_GROUPS + g) * D_STATE:SSM_INNER + (SSM_GROUPS + g + 1) * D_STATE].astype(BF16)
        cb = _dot_nt(cm, bm)
        xp = xs[:, pr * LANE:(pr + 1) * LANE]
        ms, xm, ecols, wcols, dlast, dsk = [], [], [], [], [], []
        for hh in range(2):
            hd = 2 * pr + hh
            col = cs[:, hd:hd + 1]
            rowv = cs_t[hd:hd + 1, :]
            seg = jnp.where(tril, col - rowv, -jnp.inf)
            ms.append((cb * jnp.exp(seg) * dt_t[hd:hd + 1, :]).astype(BF16))
            xm.append(jnp.where(lo if hh == 0 else ~lo, xp, 0.0).astype(BF16))
            last = cs[chunk - 1:chunk, hd:hd + 1]
            ecols.append(jnp.exp(col))
            wcols.append(jnp.exp(last - col) * dt[:, hd:hd + 1])
            dlast.append(jnp.exp(last))
            dsk.append(dskip_ref[:, hd:hd + 1])
        y_diag = _dot(jnp.concatenate(ms, axis=1), jnp.concatenate(xm, axis=0))
        st = st_ref[pr * LANE:(pr + 1) * LANE, :]
        y_off = _dot_nt(cm, st.astype(BF16)) * jnp.where(lo, ecols[0], ecols[1])
        xw = (xp * jnp.where(lo, wcols[0], wcols[1])).astype(BF16)
        st_new = _dot_tn(xw, bm)
        decay = jnp.where(rowi < SSM_HEADDIM, dlast[0], dlast[1])
        st_ref[pr * LANE:(pr + 1) * LANE, :] = decay * st + st_new
        ys.append(y_diag + y_off + jnp.where(lo, dsk[0], dsk[1]) * xp)
    y = jnp.concatenate(ys, axis=1)
    y_ref[...] = _gated_norm(y, z_ref[...], gn_ref[...])


def _ssd_chunk(xbc, z, dt, w, *, batch, seq):
    chunk = SSD_CHUNK if seq % SSD_CHUNK == 0 else seq
    nc = seq // chunk
    t = batch * seq
    eb = 2 if batch % 2 == 0 else 1
    seq_blk = lambda width: pl.BlockSpec((eb, chunk, width), lambda b, c: (b, c, 0))
    y, st = pl.pallas_call(
        functools.partial(_ssd_chunk_kernel, chunk=chunk, eb=eb),
        grid=(batch // eb, nc),
        in_specs=[seq_blk(CONV_CH), seq_blk(SSM_INNER), seq_blk(LANE), _full((CONV_K, CONV_CH)), _full((1, CONV_CH)),
                  _full((1, LANE)), _full((1, LANE)), _full((1, LANE)), _full((1, SSM_INNER))],
        out_specs=[seq_blk(SSM_INNER), pl.BlockSpec((eb, SSM_INNER, D_STATE), lambda b, c: (b, 0, 0))],
        out_shape=[jax.ShapeDtypeStruct((batch, seq, SSM_INNER), F32),
                   jax.ShapeDtypeStruct((batch, SSM_INNER, D_STATE), F32)],
        scratch_shapes=[pltpu.VMEM((eb, 8 + chunk, CONV_CH), F32)],
        compiler_params=_cparams(("parallel", "arbitrary")), name="ssd_chunk",
    )(xbc.reshape(batch, seq, CONV_CH), z.reshape(batch, seq, SSM_INNER), dt.reshape(batch, seq, LANE),
      w['conv_w'], w['conv_b'], w['dt_bias_p'], w['a_log_p'], w['d_skip_p'], w['norm_ssm_out'])
    return y.reshape(t, SSM_INNER), st


def _ssd_step_kernel(xbc_ref, cprev_ref, z_ref, dt_ref, st_ref, cw_ref, cb_ref, dtb_ref, alog_ref, dskip_ref, gn_ref,
                     y_ref, cnew_ref, stn_ref, *, nb):
    xraw = xbc_ref[...]
    conv = cb_ref[...] + xraw * cw_ref[CONV_K - 1:CONV_K, :]
    for kk in range(CONV_K - 1):
        conv = conv + cprev_ref[kk] * cw_ref[kk:kk + 1, :]
    for kk in range(CONV_K - 2):
        cnew_ref[kk] = cprev_ref[kk + 1]
    cnew_ref[CONV_K - 2] = xraw
    xbc = _silu(conv)
    xs = xbc[:, :SSM_INNER]
    dt = _softplus(dt_ref[...] + dtb_ref[...])
    decay = jnp.exp(dt * (-jnp.exp(alog_ref[...])))
    tok_l = lax.broadcasted_iota(jnp.int32, (1, nb), 1)
    heads_per_group = SSM_HEADS // SSM_GROUPS
    xs_t = xs.T
    ys = []
    for hd in range(SSM_HEADS):
        g = hd // heads_per_group
        bm = xbc[:, SSM_INNER + g * D_STATE:SSM_INNER + (g + 1) * D_STATE]
        cm = xbc[:, SSM_INNER + (SSM_GROUPS + g) * D_STATE:SSM_INNER + (SSM_GROUPS + g + 1) * D_STATE]
        x_t = xs_t[hd * SSM_HEADDIM:(hd + 1) * SSM_HEADDIM, :]
        dtb = bm * dt[:, hd:hd + 1]
        y_t = jnp.zeros((SSM_HEADDIM, nb), F32)
        for i in range(nb):
            new = decay[i:i + 1, hd:hd + 1] * st_ref[i, hd] + x_t[:, i:i + 1] * dtb[i:i + 1, :]
            stn_ref[i, hd] = new
            yi = lax.dot_general(new, cm, (((1,), (1,)), ((), ())), precision=lax.Precision.HIGHEST,
                                 preferred_element_type=F32)
            y_t = y_t + jnp.where(tok_l == i, yi, 0.0)
        ys.append(y_t)
    y = jnp.concatenate(ys, axis=0).T
    dsk = jnp.concatenate([jnp.broadcast_to(dskip_ref[:, hd:hd + 1], (1, SSM_HEADDIM)) for hd in range(SSM_HEADS)],
                          axis=1)
    y = y + dsk * xs
    y_ref[...] = _gated_norm(y, z_ref[...], gn_ref[...])


def _ssd_step(xbc, conv_prev, z, dt, state, w, *, nb):
    t = xbc.shape[0]
    row = lambda i: (i, 0)
    return pl.pallas_call(
        functools.partial(_ssd_step_kernel, nb=nb),
        grid=(t // nb,),
        in_specs=[pl.BlockSpec((nb, CONV_CH), row), pl.BlockSpec((CONV_K - 1, nb, CONV_CH), lambda i: (0, i, 0)),
                  pl.BlockSpec((nb, SSM_INNER), row), pl.BlockSpec((nb, LANE), row),
                  pl.BlockSpec((nb, SSM_HEADS, SSM_HEADDIM, D_STATE), lambda i: (i, 0, 0, 0)),
                  _full((CONV_K, CONV_CH)), _full((1, CONV_CH)), _full((1, LANE)), _full((1, LANE)), _full((1, LANE)),
                  _full((1, SSM_INNER))],
        out_specs=[pl.BlockSpec((nb, SSM_INNER), row), pl.BlockSpec((CONV_K - 1, nb, CONV_CH), lambda i: (0, i, 0)),
                   pl.BlockSpec((nb, SSM_HEADS, SSM_HEADDIM, D_STATE), lambda i: (i, 0, 0, 0))],
        out_shape=[jax.ShapeDtypeStruct((t, SSM_INNER), F32), jax.ShapeDtypeStruct((CONV_K - 1, t, CONV_CH), F32),
                   jax.ShapeDtypeStruct((t, SSM_HEADS, SSM_HEADDIM, D_STATE), F32)],
        compiler_params=_cparams(("parallel",)), name="ssd_step",
    )(xbc, conv_prev, z, dt, state, w['conv_w'], w['conv_b'], w['dt_bias_p'], w['a_log_p'], w['d_skip_p'],
      w['norm_ssm_out'])


def _out_proj_kernel(*refs, absorbed):
    refs = refs[:-4] + refs[-3:]
    if absorbed:
        (olat_ref, wuv_ref, yn_ref, x_ref, ga_ref, wout_ref, gffn_ref, wr_ref, br_ref, x1_ref, h_ref, route_ref) = refs
        o_attn = _mm(olat_ref[...], wuv_ref[...])
    else:
        (oat_ref, yn_ref, x_ref, ga_ref, wout_ref, gffn_ref, wr_ref, br_ref, x1_ref, h_ref, route_ref) = refs
        o_attn = oat_ref[...]
    n_attn = MLA_HEADS * V_DIM
    d_model = x_ref.shape[1]
    oa = _rms(o_attn, ga_ref[...])
    mix = _mm(oa, wout_ref[0:n_attn, :]) + _mm(yn_ref[...], wout_ref[n_attn:, :])
    x1 = x_ref[...] + mix
    x1_ref[...] = x1
    h = _rms(x1, gffn_ref[...])
    h_ref[:, 0, 0:d_model] = h
    if wr_ref.dtype == BF16:
        h_hi = h.astype(BF16)
        h_lo = (h - h_hi.astype(F32)).astype(BF16)
        logits = _dot(h_hi, wr_ref[0]) + (_dot(h_lo, wr_ref[0]) + _dot(h_hi, wr_ref[1])) + br_ref[...]
    else:
        logits = jnp.dot(h, wr_ref[0], precision=lax.Precision.HIGHEST, preferred_element_type=F32) + br_ref[...]
    lane = lax.broadcasted_iota(jnp.int32, logits.shape, 1)
    neg = -jnp.inf
    gl = jnp.where(lane < N_GROUPS, logits, neg)
    gmax = jnp.max(gl, axis=-1, keepdims=True)
    g_w = 1.0 / jnp.sum(jnp.exp(gl - gmax), axis=-1, keepdims=True)
    g_idx = jnp.min(jnp.where(gl == gmax, lane, LANE), axis=-1, keepdims=True)
    e_lo = N_GROUPS + EXPERTS_PER_GROUP * g_idx
    el = jnp.where((lane >= e_lo) & (lane < e_lo + EXPERTS_PER_GROUP), logits, neg)
    e1 = jnp.max(el, axis=-1, keepdims=True)
    i1 = jnp.min(jnp.where(el == e1, lane, LANE), axis=-1, keepdims=True)
    el2 = jnp.where(lane == i1, neg, el)
    e2 = jnp.max(el2, axis=-1, keepdims=True)
    i2 = jnp.min(jnp.where(el2 == e2, lane, LANE), axis=-1, keepdims=True)
    r = jnp.exp(e2 - e1)
    c1 = g_w / (1.0 + r)
    c2 = g_w * r / (1.0 + r)
    first = i1 < i2
    ia = jnp.where(first, i1, i2)
    ib = jnp.where(first, i2, i1)
    ca = jnp.where(first, c1, c2)
    cb = jnp.where(first, c2, c1)
    la = ia - e_lo
    lb = ib - e_lo
    pair = jnp.where(la == 0, 0, jnp.where(la == 1, 3, 5)) + (lb - la - 1)
    bucket = g_idx * PAIRS_PER_GROUP + pair
    route = jnp.where(lane == R_EA, (ia - N_GROUPS).astype(F32), 0.0)
    route = jnp.where(lane == R_EB, (ib - N_GROUPS).astype(F32), route)
    route = jnp.where(lane == R_CA, ca, route)
    route = jnp.where(lane == R_CB, cb, route)
    route = jnp.where(lane == R_BUCKET, bucket.astype(F32), route)
    route_ref[...] = route
    h_ref[:, 0, d_model:d_model + LANE] = route


def _out_proj(o_in, yn, x, w, *, absorbed, tm, h_buf, h_row0):
    t, d = x.shape
    row = lambda i: (i, 0)
    assert h_row0 % tm == 0
    h_blk0 = h_row0 // tm
    n_mix = MLA_HEADS * V_DIM + SSM_INNER
    if absorbed:
        first = [pl.BlockSpec((tm, MLA_HEADS * KV_LORA), row), _full((MLA_HEADS * KV_LORA, MLA_HEADS * V_DIM))]
        args = [o_in, w['w_uv_bd32']]
    else:
        first = [pl.BlockSpec((tm, MLA_HEADS * V_DIM), row)]
        args = [o_in]
    w_out = w['w_out32'] if absorbed else w['w_out']
    in_specs = first + [pl.BlockSpec((tm, SSM_INNER), row), pl.BlockSpec((tm, d), row), _full((1, MLA_HEADS * V_DIM)),
                        _full((n_mix, d)), _full((1, d)), _full((2, d, LANE)), _full((1, LANE))]
    args += [yn, x, w['norm_attn_out'], w_out, w['norm_ffn'], w['w_router32' if absorbed else 'w_router'],
             w['b_router']]
    in_specs.append(pl.BlockSpec(memory_space=pl.ANY))
    aliases = {len(args): 1}
    args.append(h_buf)
    return pl.pallas_call(
        functools.partial(_out_proj_kernel, absorbed=absorbed),
        grid=(t // tm,), in_specs=in_specs,
        out_specs=[pl.BlockSpec((tm, d), row), pl.BlockSpec((tm, 1, d + LANE), lambda i: (h_blk0 + i, 0, 0)),
                   pl.BlockSpec((tm, LANE), row)],
        out_shape=[jax.ShapeDtypeStruct((t, d), F32), jax.ShapeDtypeStruct(h_buf.shape, F32),
                   jax.ShapeDtypeStruct((t, LANE), F32)],
        input_output_aliases=aliases,
        compiler_params=_cparams(("parallel",)), name="out_proj_abs" if absorbed else "out_proj",
    )(*args)


def _row_copy(src_hbm, src_row, dst, r, sem):
    return pltpu.make_async_copy(src_hbm.at[src_row], dst.at[pl.ds(r, 1), :], sem)


ROW_GROUP = 32


def _row_groups(n_rows, n_valid, body):
    for r0 in range(0, n_rows, ROW_GROUP):
        def group(r0=r0):
            for r in range(r0, min(r0 + ROW_GROUP, n_rows)):
                body(r)
        if n_valid is None:
            group()
        else:
            pl.when(r0 < n_valid)(group)


def _row_gather(idx_ref, src_hbm, dst, sem, n_rows, n_valid):
    _row_groups(n_rows, n_valid, lambda r: _row_copy(src_hbm, idx_ref[0, 0, r], dst, r, sem).start())


def _row_gather_wait(src_hbm, dst, sem, n_rows, n_valid):
    _row_groups(n_rows, n_valid, lambda r: _row_copy(src_hbm, 0, dst, r, sem).wait())


def _gather_begin(i, idx_ref, src_hbm, buf, sem, n_rows, n_valid=None):
    slot = i % 2

    @pl.when(i == 0)
    def _():
        _row_gather(idx_ref, src_hbm, buf.at[0], sem.at[0], n_rows, n_valid)

    _row_gather_wait(src_hbm, buf.at[slot], sem.at[slot], n_rows, n_valid)
    return slot


def _gather_next(slot, idxn_ref, src_hbm, buf, sem, n_rows, n_valid=None):
    _row_gather(idxn_ref, src_hbm, buf.at[1 - slot], sem.at[1 - slot], n_rows, n_valid)


def _gather_drain(i, n, src_hbm, buf, sem, n_rows, n_valid=None):
    @pl.when(i == n - 1)
    def _():
        other = 1 - i % 2
        _row_gather_wait(src_hbm, buf.at[other], sem.at[other], n_rows, n_valid)


def _moe_kernel(ta_ref, tb_ref, tv_ref, tok_ref, tokn_ref, h_hbm, wga_ref, wua_ref, wda_ref, wgb_ref, wub_ref,
                wdb_ref, y_ref, xbuf, sem, *, tm):
    del ta_ref, tb_ref
    i = pl.program_id(0)
    n = pl.num_programs(0)
    n_valid = tv_ref[i]
    n_valid_next = tv_ref[jnp.minimum(i + 1, n - 1)]

    @pl.when(i == 0)
    def _():
        xbuf[...] = jnp.zeros(xbuf.shape, F32)

    slot = _gather_begin(i, tok_ref, h_hbm, xbuf, sem, tm, n_valid)
    _gather_next(slot, tokn_ref, h_hbm, xbuf, sem, tm, n_valid_next)
    d = y_ref.shape[2]

    @pl.when(n_valid > 0)
    def _():
        x = xbuf[slot, :, 0:d].astype(BF16)
        rec = xbuf[slot, :, d:d + LANE]
        y = None
        for (wg_ref, wu_ref, wd_ref), lane_c in (((wga_ref, wua_ref, wda_ref), R_CA),
                                                 ((wgb_ref, wub_ref, wdb_ref), R_CB)):
            act = _silu(_dot(x, wg_ref[0])) * _dot(x, wu_ref[0])
            term = rec[:, lane_c:lane_c + 1] * _dot(act.astype(BF16), wd_ref[0])
            y = term if y is None else y + term
        y_ref[:, 0, :] = y

    @pl.when(n_valid == 0)
    def _():
        y_ref[...] = jnp.zeros(y_ref.shape, F32)

    _gather_drain(i, n, h_hbm, xbuf, sem, tm, n_valid_next)


def _moe(h, tile_ea, tile_eb, tile_valid, row_token, w, *, tm):
    t, _, dp = h.shape
    d = dp - LANE
    n_tiles = tile_ea.shape[0]
    ea = lambda i, ta, tb, tv: (ta[i], 0, 0)
    eb = lambda i, ta, tb, tv: (tb[i], 0, 0)
    grid_spec = pltpu.PrefetchScalarGridSpec(
        num_scalar_prefetch=3, grid=(n_tiles,),
        in_specs=[pl.BlockSpec((1, 1, tm), lambda i, ta, tb, tv: (i, 0, 0), memory_space=pltpu.SMEM),
                  pl.BlockSpec((1, 1, tm), lambda i, ta, tb, tv: (jnp.minimum(i + 1, n_tiles - 1), 0, 0),
                               memory_space=pltpu.SMEM),
                  pl.BlockSpec(memory_space=pl.ANY),
                  pl.BlockSpec((1, d, EXPERT_FF), ea), pl.BlockSpec((1, d, EXPERT_FF), ea),
                  pl.BlockSpec((1, EXPERT_FF, d), ea),
                  pl.BlockSpec((1, d, EXPERT_FF), eb), pl.BlockSpec((1, d, EXPERT_FF), eb),
                  pl.BlockSpec((1, EXPERT_FF, d), eb)],
        out_specs=pl.BlockSpec((tm, 1, d), lambda i, ta, tb, tv: (i, 0, 0)),
        scratch_shapes=[pltpu.VMEM((2, tm, dp), F32), pltpu.SemaphoreType.DMA((2,))])
    rt = row_token.reshape(n_tiles, 1, tm)
    wts = (w['w_exp_gate'], w['w_exp_up'], w['w_exp_down'])
    return pl.pallas_call(
        functools.partial(_moe_kernel, tm=tm), grid_spec=grid_spec,
        out_shape=jax.ShapeDtypeStruct((n_tiles * tm, 1, d), F32),
        compiler_params=_cparams(("arbitrary",)), name="moe",
    )(tile_ea, tile_eb, tile_valid, rt, rt, h, *wts, *wts)


def _moe_plan(route, *, tm):
    t = route.shape[0]
    bids = route[:, R_BUCKET].astype(jnp.int32)
    onehot = (bids[:, None] == jnp.arange(N_BUCKETS, dtype=jnp.int32)[None, :]).astype(F32)
    blk = _tile(t, LANE)
    oh = onehot.reshape(-1, blk, N_BUCKETS)
    intra = jnp.einsum('ts,bse->bte', jnp.tril(jnp.ones((blk, blk), F32)), oh)
    tot = intra[:, -1, :]
    csum = (intra + (jnp.cumsum(tot, axis=0) - tot)[:, None, :]).reshape(-1, N_BUCKETS)
    rank = jnp.sum(onehot * (csum - 1.0), axis=1).astype(jnp.int32)
    counts = csum[-1].astype(jnp.int32)
    padded = ((counts + tm - 1) // tm) * tm
    ends = jnp.cumsum(padded)
    offs = ends - padded
    pos = offs[bids] + rank
    n_tiles = -(-t // tm) + N_BUCKETS
    row_token = jnp.zeros((n_tiles * tm,), jnp.int32).at[pos].set(jnp.arange(t, dtype=jnp.int32))
    tile_start = jnp.arange(n_tiles, dtype=jnp.int32) * tm
    tile_bucket = jnp.sum((ends[None, :] <= tile_start[:, None]).astype(jnp.int32), axis=1)
    tile_bucket = jnp.minimum(tile_bucket, N_BUCKETS - 1)
    group0 = (tile_bucket // PAIRS_PER_GROUP) * EXPERTS_PER_GROUP
    pair = tile_bucket % PAIRS_PER_GROUP
    tile_ea = group0 + jnp.asarray(PAIR_A, jnp.int32)[pair]
    tile_eb = group0 + jnp.asarray(PAIR_B, jnp.int32)[pair]
    tile_valid = jnp.clip((offs + counts)[tile_bucket] - tile_start, 0, tm).astype(jnp.int32)
    return tile_ea, tile_eb, tile_valid, row_token, pos


def _final_kernel(pos_ref, posn_ref, x1_ref, p_ref, y_hbm, gple_ref, wgate_ref, wproj_ref, gfin_ref,
                  out_ref, ybuf, sem, *, tm):
    i = pl.program_id(0)
    n = pl.num_programs(0)
    slot = _gather_begin(i, pos_ref, y_hbm, ybuf, sem, tm)
    _gather_next(slot, posn_ref, y_hbm, ybuf, sem, tm)
    x2 = x1_ref[...] + ybuf[slot]
    gate_in = _rms(x2, gple_ref[...]).astype(BF16)
    gate = 1.0 / (1.0 + jnp.exp(-_dot(gate_in, wgate_ref[...])))
    x3 = x2 + gate * _dot(p_ref[...].astype(BF16), wproj_ref[...])
    out_ref[...] = _rms(x3, gfin_ref[...])
    _gather_drain(i, n, y_hbm, ybuf, sem, tm)


def _final(x1, p, y_sorted, pos, w, *, tm):
    t, d = x1.shape
    nt = t // tm
    ple = p.shape[1]
    pos_t = pos.reshape(nt, 1, tm)
    row = lambda i: (i, 0)
    in_specs = [pl.BlockSpec((1, 1, tm), lambda i: (i, 0, 0), memory_space=pltpu.SMEM),
                pl.BlockSpec((1, 1, tm), lambda i: (jnp.minimum(i + 1, nt - 1), 0, 0), memory_space=pltpu.SMEM),
                pl.BlockSpec((tm, d), row), pl.BlockSpec((tm, ple), row),
                pl.BlockSpec(memory_space=pl.ANY), _full((1, d)), _full((d, d)), _full((ple, d)), _full((1, d))]
    return pl.pallas_call(
        functools.partial(_final_kernel, tm=tm), grid=(nt,), in_specs=in_specs,
        out_specs=pl.BlockSpec((tm, d), row), out_shape=jax.ShapeDtypeStruct((t, d), F32),
        scratch_shapes=[pltpu.VMEM((2, tm, d), F32), pltpu.SemaphoreType.DMA((2,))],
        compiler_params=_cparams(("arbitrary",)), name="final",
    )(pos_t, pos_t, x1, p, y_sorted, w['norm_ple'], w['w_ple_gate'], w['w_ple_proj'], w['norm_final'])


def _pad_lanes(v, width=LANE):
    v = v.reshape(1, -1)
    return jnp.pad(v, ((0, 0), (0, width - v.shape[1])))


def _rot_cols(wr):
    half = QK_ROPE // 2
    return jnp.concatenate([-wr[..., half:], wr[..., :half]], axis=-1)


def _prep_weights(norm_mix, w_in, norm_q, w_uq, norm_kv, w_ukv, norm_attn_out, conv_w, conv_b, dt_bias, a_log, d_skip,
                  norm_ssm_out, w_out, norm_ffn, w_group_router, b_group_router, w_expert_router, b_expert_router,
                  w_exp_gate, w_exp_up, w_exp_down, norm_ple, w_ple_gate, w_ple_proj, norm_final):
    d = w_in.shape[0]
    splits = [Q_LORA, KV_LORA, QK_ROPE, SSM_INNER, CONV_CH, SSM_HEADS]
    bounds = [0]
    for s in splits:
        bounds.append(bounds[-1] + s)
    w_cq, w_ckv, w_kr, w_z, w_xbc, w_dt = [w_in[:, bounds[j]:bounds[j + 1]] for j in range(6)]
    zpad = lambda n: jnp.zeros((d, n), F32)
    place = lambda m: jnp.concatenate([zpad(ROPE_LANE0), m, zpad(LANE - ROPE_LANE0 - QK_ROPE)], axis=1)
    w_in_p = jnp.concatenate([w_cq, w_ckv, place(w_kr), place(_rot_cols(w_kr)), w_z, w_xbc, w_dt,
                              zpad(LANE - SSM_HEADS)], axis=1)
    nope, rp = w_uq[..., :QK_NOPE], w_uq[..., QK_NOPE:]
    zq = lambda n: jnp.zeros((Q_LORA, MLA_HEADS, n), F32)
    tail = HEAD_PAD - QK_NOPE - QK_ROPE
    wq_a = jnp.concatenate([nope, rp, zq(tail)], axis=-1).reshape(Q_LORA, -1)
    wq_b = jnp.concatenate([zq(QK_NOPE), _rot_cols(rp), zq(tail)], axis=-1).reshape(Q_LORA, -1)
    w_uk, w_uv = w_ukv[..., :QK_NOPE], w_ukv[..., QK_NOPE:]
    wk_p = jnp.concatenate([w_uk, jnp.zeros((KV_LORA, MLA_HEADS, HEAD_PAD - QK_NOPE), F32)], axis=-1)
    w_kv_p = jnp.concatenate([wk_p.reshape(KV_LORA, -1), w_uv.reshape(KV_LORA, -1)], axis=1).astype(BF16)
    w_abs = jnp.zeros((MLA_HEADS, HEAD_PAD, Q_ABS), F32)
    w_abs = w_abs.at[:, :QK_NOPE, :KV_LORA].set(jnp.transpose(w_uk, (1, 2, 0)))
    w_abs = w_abs.at[:, QK_NOPE:QK_NOPE + QK_ROPE, KV_LORA:KV_LORA + QK_ROPE].set(
        jnp.broadcast_to(jnp.eye(QK_ROPE, dtype=F32), (MLA_HEADS, QK_ROPE, QK_ROPE)))
    w_uv_bd = jnp.zeros((MLA_HEADS, KV_LORA, MLA_HEADS, V_DIM), F32)
    hidx = jnp.arange(MLA_HEADS)
    w_uv_bd = w_uv_bd.at[hidx, :, hidx, :].set(jnp.transpose(w_uv, (1, 0, 2)))
    w_router = jnp.concatenate([w_group_router, w_expert_router,
                                jnp.zeros((d, LANE - N_GROUPS - N_EXPERTS), F32)], axis=1)
    b_router = _pad_lanes(jnp.concatenate([b_group_router, b_expert_router]))
    w_router_hi = w_router.astype(BF16)
    w_router_lo = (w_router - w_router_hi.astype(F32)).astype(BF16)
    w_router32 = jnp.stack([w_router, jnp.zeros_like(w_router)])
    w_router = jnp.stack([w_router_hi, w_router_lo])
    return dict(
        norm_mix=norm_mix.reshape(1, -1), w_in_p=w_in_p.astype(BF16), w_in_p32=w_in_p, norm_q=norm_q.reshape(1, -1),
        wq_a=wq_a.astype(BF16), wq_b=wq_b.astype(BF16), wq_a32=wq_a, wq_b32=wq_b,
        norm_kv=norm_kv.reshape(1, -1), w_kv_p=w_kv_p, w_abs32=w_abs,
        w_uv_bd32=w_uv_bd.reshape(MLA_HEADS * KV_LORA, MLA_HEADS * V_DIM),
        norm_attn_out=norm_attn_out.reshape(1, -1), conv_w=conv_w, conv_b=conv_b.reshape(1, -1),
        dt_bias_p=_pad_lanes(dt_bias), a_log_p=_pad_lanes(a_log), d_skip_p=_pad_lanes(d_skip),
        norm_ssm_out=norm_ssm_out.reshape(1, -1), w_out=w_out.astype(BF16), w_out32=w_out,
        norm_ffn=norm_ffn.reshape(1, -1),
        w_router=w_router, w_router32=w_router32, b_router=b_router, w_exp_gate=w_exp_gate.astype(BF16), w_exp_up=w_exp_up.astype(BF16),
        w_exp_down=w_exp_down.astype(BF16), norm_ple=norm_ple.reshape(1, -1), w_ple_gate=w_ple_gate.astype(BF16),
        w_ple_proj=w_ple_proj.astype(BF16), norm_final=norm_final.reshape(1, -1))


def _rope_tables(pos):
    half = QK_ROPE // 2
    inv = ROPE_THETA ** (-jnp.arange(half, dtype=F32) / half)
    ang = pos.astype(F32)[:, None] * inv[None, :]
    place = lambda m: jnp.pad(jnp.concatenate([m, m], axis=1), ((0, 0), (ROPE_LANE0, LANE - ROPE_LANE0 - QK_ROPE)))
    return place(jnp.cos(ang)), place(jnp.sin(ang))


def _tile(n, pref):
    return pref if n % pref == 0 else n


def kernel(x_prompt, x_sample, p_prompt, p_sample, cache_ckv, cache_krope, state_conv, state_ssm, page_table, norm_mix, w_in, norm_q, w_uq, norm_kv, w_ukv, norm_attn_out, conv_w, conv_b, dt_bias, a_log, d_skip, norm_ssm_out, w_out, norm_ffn, w_group_router, b_group_router, w_expert_router, b_expert_router, w_exp_gate, w_exp_up, w_exp_down, norm_ple, w_ple_gate, w_ple_proj, norm_final):
    depth = w_in.shape[0]
    assert depth == 1, "single-layer stack"
    bp, sp, d = x_prompt.shape
    bs, ls, _ = x_sample.shape
    assert ls == 1, "decode step handles one new token per sequence"
    n_pool, page, _ = cache_ckv.shape[1:]
    n_pages = page_table.shape[1]
    past_len = n_pages * page
    w = _prep_weights(norm_mix[0], w_in[0], norm_q[0], w_uq[0], norm_kv[0], w_ukv[0], norm_attn_out[0], conv_w[0],
                      conv_b[0], dt_bias[0], a_log[0], d_skip[0], norm_ssm_out[0], w_out[0], norm_ffn[0],
                      w_group_router[0], b_group_router[0], w_expert_router[0], b_expert_router[0], w_exp_gate[0],
                      w_exp_up[0], w_exp_down[0], norm_ple[0], w_ple_gate[0], w_ple_proj[0], norm_final)

    tp = bp * sp
    xp = x_prompt.reshape(tp, d)
    tm_p = _tile(sp, 512)
    cos_p, sin_p = _rope_tables(jnp.arange(sp))
    q, k, v, ckv_p, krp_p, z_p, xbc_p, dt_p = _in_proj(xp, cos_p, sin_p, w, absorb=False, tm=tm_p,
                                                       n_pos_blocks=sp // tm_p)
    tq = _tile(sp, 1024)
    o_attn = _flash(q, k, v, batch=bp, seq=sp, tq=tq, tk=tq)
    yn_p, ssm_p = _ssd_chunk(xbc_p, z_p, dt_p, w, batch=bp, seq=sp)
    h_all = jnp.zeros((tp + bs, 1, d + LANE), F32)
    x1_p, h_all, route_p = _out_proj(o_attn, yn_p, xp, w, absorbed=False, tm=_tile(tp, 512), h_buf=h_all, h_row0=0)

    xs = x_sample.reshape(bs, d)
    cos_s, sin_s = _rope_tables(jnp.full((bs,), past_len, jnp.int32))
    q_abs, ckv_s, krp_s, z_s, xbc_s, dt_s = _in_proj(xs, cos_s, sin_s, w, absorb=True, tm=bs, n_pos_blocks=1)
    ch = n_pages if n_pages <= 128 else (64 if n_pages % 64 == 0 else n_pages)
    o_lat = _paged(q_abs.reshape(bs, MLA_HEADS, Q_ABS), ckv_s.reshape(bs, 1, KV_LORA), krp_s.reshape(bs, 1, LANE),
                   cache_ckv.reshape(n_pool, page, KV_LORA), jnp.swapaxes(cache_krope, 2, 3).reshape(n_pool, QK_ROPE, page), page_table,
                   ch=ch, nslot=2, eb=1)
    conv_prev = jnp.transpose(state_conv[0], (1, 0, 2))
    yn_s, conv_new, ssm_s = _ssd_step(xbc_s, conv_prev, z_s, dt_s, state_ssm[0], w, nb=_tile(bs, 8))
    x1_s, h_all, route_s = _out_proj(o_lat.reshape(bs, MLA_HEADS * KV_LORA), yn_s, xs, w, absorbed=True, tm=bs,
                                     h_buf=h_all, h_row0=tp)

    tm_moe = 256
    tile_ea, tile_eb, tile_valid, row_token, pos = _moe_plan(jnp.concatenate([route_p, route_s], axis=0), tm=tm_moe)
    y_sorted = _moe(h_all, tile_ea, tile_eb, tile_valid, row_token, w, tm=tm_moe)
    y_prompt = _final(x1_p, p_prompt[0].reshape(tp, -1), y_sorted, pos[:tp], w, tm=_tile(tp, 512))
    y_sample = _final(x1_s, p_sample[0].reshape(bs, -1), y_sorted, pos[tp:], w, tm=bs)

    rope_sl = slice(ROPE_LANE0, ROPE_LANE0 + QK_ROPE)
    return (y_prompt.reshape(bp, sp, d), y_sample.reshape(bs, ls, d),
            ckv_p.reshape(1, bp, sp, KV_LORA), krp_p[:, rope_sl].reshape(1, bp, sp, QK_ROPE),
            xbc_p.reshape(bp, sp, CONV_CH)[:, sp - (CONV_K - 1):, :][None],
            ssm_p.reshape(1, bp, SSM_HEADS, SSM_HEADDIM, D_STATE),
            ckv_s.reshape(1, bs, ls, KV_LORA), krp_s[:, rope_sl].reshape(1, bs, ls, QK_ROPE),
            jnp.transpose(conv_new, (1, 0, 2))[None], ssm_s[None])
```
